```python
import jax, jax.numpy as jnp
from jax import lax
import numpy as np

D_MODEL = 1024
BATCH = 2
SEQ = 8192
DEPTH = 4
DEC_BATCH = 128
DEC_SEQ = 8
PAST_LEN = 2048
PAGE_SIZE = 128

N_MIXERS = 2
N_RG = (DEPTH + 1) // 2
N_NSA = DEPTH // 2
D_RNN = D_MODEL
RG_BLOCKS = 4
RG_BD = D_RNN // RG_BLOCKS
CONV_W = 4
RG_C = 8.0
N_HEADS = 16
HEAD_DIM = D_MODEL // N_HEADS
KV_GROUPS = 4
GROUP_SIZE = N_HEADS // KV_GROUPS
ROT_DIM = HEAD_DIM // 4
ROPE_THETA = 500000.0
CMP_BLOCK = 32
CMP_STRIDE = 16
CMP_HIDDEN = 2 * HEAD_DIM
SEL_BLOCK = 64
SEL_TOP = 16
WINDOW = 512
Q_BLOCK = 128
NSA_Q = N_HEADS * HEAD_DIM
NSA_KV = KV_GROUPS * HEAD_DIM
NSA_PROJ = NSA_Q + 6 * NSA_KV + 3 * N_HEADS
N_EXPERTS = 32
TOP_K = 4
D_FF = D_MODEL
SWIGLU_LIMIT = 7.0
SWIGLU_ALPHA = 1.702
GROUP_ROWS = 128
PLE_DIM = 256
DN_ALPHA = (2 * DEPTH) ** 0.25
DN_BETA = (8 * DEPTH) ** -0.25
LN_EPS = 1e-5
NEG = -1e30
BIG = 1e30

kernel_name = 'hybrid_rglru_nsa_moe_decode_step'


def layer_norm(x, g, b):
    xf = x.astype(jnp.float32)
    mu = jnp.mean(xf, -1, keepdims=True)
    var = jnp.mean(jnp.square(xf - mu), -1, keepdims=True)
    return ((xf - mu) * lax.rsqrt(var + LN_EPS) * g.astype(jnp.float32) + b.astype(jnp.float32)).astype(x.dtype)


def partial_rope(x, pos):
    half = ROT_DIM // 2
    inv = ROPE_THETA ** (-jnp.arange(half, dtype=jnp.float32) / half)
    ang = pos.astype(jnp.float32)[:, None] * inv[None, :]
    cos = jnp.cos(ang)[None, :, None, :]
    sin = jnp.sin(ang)[None, :, None, :]
    xr = x[..., :ROT_DIM].astype(jnp.float32)
    x1, x2 = xr[..., :half], xr[..., half:]
    rot = jnp.concatenate([x1 * cos - x2 * sin, x2 * cos + x1 * sin], -1).astype(x.dtype)
    return jnp.concatenate([rot, x[..., ROT_DIM:]], -1)


def _lru_combine(c1, c2):
    a1, b1 = c1
    a2, b2 = c2
    return a1 * a2, a2 * b1 + b2


def rglru_mixer(x, conv_buf, h0, w_in, conv_w, conv_b, w_a, b_a, w_i, b_i, lam, w_o):
    f32 = jnp.float32
    B, T, _ = x.shape
    proj = x @ w_in
    y = jax.nn.gelu(proj[..., :D_RNN])
    u = proj[..., D_RNN:]
    ucat = jnp.concatenate([conv_buf.astype(u.dtype), u], axis=1)
    conv = conv_b + sum(ucat[:, k:k + T] * conv_w[k] for k in range(CONV_W))
    cb = conv.reshape(B, T, RG_BLOCKS, RG_BD)
    r = jax.nn.sigmoid(jnp.einsum('btnd,nde->btne', cb, w_a, preferred_element_type=f32).reshape(B, T, D_RNN) + b_a.astype(f32))
    ig = jax.nn.sigmoid(jnp.einsum('btnd,nde->btne', cb, w_i, preferred_element_type=f32).reshape(B, T, D_RNN) + b_i.astype(f32))
    log_a = -RG_C * r * jax.nn.softplus(-lam.astype(f32))
    a = jnp.exp(log_a)
    b = jnp.sqrt(-jnp.expm1(2.0 * log_a)) * ig * conv.astype(f32)
    a_cum, h_zero = lax.associative_scan(_lru_combine, (a, b), axis=1)
    h = h_zero + a_cum * h0.astype(f32)[:, None, :]
    out = (h.astype(x.dtype) * y) @ w_o
    return out, ucat[:, T:], h[:, -1]


def nsa_project(x, pos, w_in):
    B, T, _ = x.shape
    proj = x @ w_in
    q = partial_rope(proj[..., :NSA_Q].reshape(B, T, N_HEADS, HEAD_DIM), pos)
    kv = proj[..., NSA_Q:NSA_Q + 6 * NSA_KV].reshape(B, T, 3, 2, KV_GROUPS, HEAD_DIM)
    k = partial_rope(kv[:, :, :, 0].reshape(B, T, 3 * KV_GROUPS, HEAD_DIM), pos).reshape(B, T, 3, KV_GROUPS, HEAD_DIM)
    kv = jnp.stack([k, kv[:, :, :, 1]], axis=3).reshape(B, T, 6, KV_GROUPS, HEAD_DIM)
    g = jax.nn.sigmoid(proj[..., NSA_Q + 6 * NSA_KV:].astype(jnp.float32)).reshape(B, T, N_HEADS, 3)
    return q, kv[:, :, :4], kv[:, :, 4:], g


def compress_blocks(k, pe, w1, w2):
    B, T, G, Dh = k.shape
    n_c = (T - CMP_BLOCK) // CMP_STRIDE + 1
    idx = jnp.arange(n_c)[:, None] * CMP_STRIDE + jnp.arange(CMP_BLOCK)[None, :]
    kb = k[:, idx] + pe[None, None, :, None, :]
    kb = jnp.transpose(kb, (0, 1, 3, 2, 4)).reshape(B, n_c, G, CMP_BLOCK * Dh)
    return jax.nn.gelu(kb @ w1) @ w2


def sel_blocks(k):
    B, T, G, Dh = k.shape
    pad = (-T) % SEL_BLOCK
    k = jnp.pad(k, ((0, 0), (0, pad), (0, 0), (0, 0)))
    return jnp.transpose(k.reshape(B, (T + pad) // SEL_BLOCK, SEL_BLOCK, G, Dh), (0, 3, 1, 2, 4))


def overlap_matrix(n_c, n_sel):
    c0 = jnp.arange(n_c)[:, None] * CMP_STRIDE
    j0 = jnp.arange(n_sel)[None, :] * SEL_BLOCK
    return ((c0 < j0 + SEL_BLOCK) & (c0 + CMP_BLOCK > j0)).astype(jnp.float32)


def nsa_attend(q, pos_q, g, kc, vc, ks_b, vs_b, kw, vw, pos_w):
    f32 = jnp.float32
    B, Tq = q.shape[:2]
    n_c = kc.shape[1]
    n_sel = ks_b.shape[2]
    scale = HEAD_DIM ** -0.5
    qg = q.reshape(B, Tq, KV_GROUPS, GROUP_SIZE, HEAD_DIM)
    s = jnp.einsum('bqgrd,bcgd->bgrqc', qg, kc, preferred_element_type=f32) * scale
    cmp_end = jnp.arange(n_c) * CMP_STRIDE + CMP_BLOCK - 1
    m_c = cmp_end[None, :] <= pos_q[:, None]
    p_c = jax.nn.softmax(jnp.where(m_c, s, NEG), axis=-1) * m_c
    o_c = jnp.einsum('bgrqc,bcgd->bqgrd', p_c, vc.astype(f32))
    imp = jnp.einsum('bgrqc,cj->bgqj', p_c, overlap_matrix(n_c, n_sel))
    jq = pos_q // SEL_BLOCK
    jb = jnp.arange(n_sel)[None, :]
    forced = (jb == 0) | (jb == jq[:, None]) | (jb == jq[:, None] - 1)
    imp = jnp.where(forced, BIG, imp)
    imp = jnp.where(jb > jq[:, None], -BIG, imp)
    _, top = lax.top_k(imp, min(SEL_TOP, n_sel))
    gather = jax.vmap(jax.vmap(lambda blocks, ix: blocks[ix]))
    n_k = top.shape[-1] * SEL_BLOCK
    k_sel = gather(ks_b, top).reshape(B, KV_GROUPS, Tq, n_k, HEAD_DIM)
    v_sel = gather(vs_b, top).reshape(B, KV_GROUPS, Tq, n_k, HEAD_DIM)
    pos_sel = (top[..., None] * SEL_BLOCK + jnp.arange(SEL_BLOCK)).reshape(B, KV_GROUPS, Tq, n_k)
    m_s = (pos_sel <= pos_q[None, None, :, None])[:, :, None]
    s = jnp.einsum('bqgrd,bgqkd->bgrqk', qg, k_sel, preferred_element_type=f32) * scale
    p_s = jax.nn.softmax(jnp.where(m_s, s, NEG), axis=-1)
    o_s = jnp.einsum('bgrqk,bgqkd->bqgrd', p_s, v_sel.astype(f32))
    d = pos_q[:, None] - pos_w[None, :]
    m_w = (pos_w[None, :] >= 0) & (d >= 0) & (d <= WINDOW)
    s = jnp.einsum('bqgrd,bkgd->bgrqk', qg, kw, preferred_element_type=f32) * scale
    p_w = jax.nn.softmax(jnp.where(m_w, s, NEG), axis=-1)
    o_w = jnp.einsum('bgrqk,bkgd->bqgrd', p_w, vw.astype(f32))
    gg = g.reshape(B, Tq, KV_GROUPS, GROUP_SIZE, 3)
    o = gg[..., 0:1] * o_c + gg[..., 1:2] * o_s + gg[..., 2:3] * o_w
    return o.reshape(B, Tq, NSA_Q).astype(q.dtype)


def nsa_prompt(x, w_in, pe_k, ck_w1, ck_w2, pe_v, cv_w1, cv_w2, w_o):
    B, T, _ = x.shape
    w_buf = min(WINDOW, PAST_LEN)
    q, kv, win, g = nsa_project(x, jnp.arange(T), w_in)
    kc = compress_blocks(kv[:, :, 0], pe_k, ck_w1, ck_w2)
    vc = compress_blocks(kv[:, :, 1], pe_v, cv_w1, cv_w2)
    ks_b = sel_blocks(kv[:, :, 2])
    vs_b = sel_blocks(kv[:, :, 3])
    win_pad = jnp.pad(win, ((0, 0), (WINDOW, 0), (0, 0), (0, 0), (0, 0)))

    def q_block(qb):
        start = qb * Q_BLOCK
        q_blk = lax.dynamic_slice_in_dim(q, start, Q_BLOCK, axis=1)
        g_blk = lax.dynamic_slice_in_dim(g, start, Q_BLOCK, axis=1)
        w_blk = lax.dynamic_slice_in_dim(win_pad, start, WINDOW + Q_BLOCK, axis=1)
        pos_q = start + jnp.arange(Q_BLOCK)
        pos_w = start - WINDOW + jnp.arange(WINDOW + Q_BLOCK)
        return nsa_attend(q_blk, pos_q, g_blk, kc, vc, ks_b, vs_b, w_blk[:, :, 0], w_blk[:, :, 1], pos_w)

    o = lax.map(q_block, jnp.arange(T // Q_BLOCK))
    o = jnp.moveaxis(o, 0, 1).reshape(B, T, NSA_Q)
    new_win = win[:, max(T - w_buf, 0):]
    if T < w_buf:
        new_win = jnp.pad(new_win, ((0, 0), (w_buf - T, 0), (0, 0), (0, 0), (0, 0)))
    return o @ w_o, kv, new_win


def nsa_sample(x, cache_kv, li, page_table, win_buf, w_in, pe_k, ck_w1, ck_w2, pe_v, cv_w1, cv_w2, w_o):
    B, T, _ = x.shape
    pos = PAST_LEN + jnp.arange(T)
    q, kv, win, g = nsa_project(x, pos, w_in)
    past = cache_kv[li, page_table].reshape(B, -1, 4, KV_GROUPS, HEAD_DIM)
    full = jnp.concatenate([past.astype(kv.dtype), kv], axis=1)
    kc = compress_blocks(full[:, :, 0], pe_k, ck_w1, ck_w2)
    vc = compress_blocks(full[:, :, 1], pe_v, cv_w1, cv_w2)
    ks_b = sel_blocks(full[:, :, 2])
    vs_b = sel_blocks(full[:, :, 3])
    wcat = jnp.concatenate([win_buf.astype(win.dtype), win], axis=1)
    w_buf = win_buf.shape[1]
    pos_w = PAST_LEN - w_buf + jnp.arange(w_buf + T)
    o = nsa_attend(q, pos, g, kc, vc, ks_b, vs_b, wcat[:, :, 0], wcat[:, :, 1], pos_w)
    return o @ w_o, kv, wcat[:, T:]


def moe_ffn(x, w_r, b_r, w_gu, b_gu, w_dn, b_dn):
    f32 = jnp.float32
    B, T, D = x.shape
    n_tok = B * T
    xt = x.reshape(n_tok, D)
    logits = jnp.einsum('nd,de->ne', xt, w_r, preferred_element_type=f32) + b_r.astype(f32)
    top_val, top_idx = lax.top_k(logits, TOP_K)
    gates = jax.nn.softmax(top_val, axis=-1)
    n_as = n_tok * TOP_K
    e_flat = top_idx.reshape(n_as)
    tok_flat = jnp.arange(n_as, dtype=jnp.int32) // TOP_K
    order = jnp.argsort(e_flat)
    e_s = e_flat[order]
    tok_s = tok_flat[order]
    g_s = gates.reshape(n_as)[order]
    counts = jnp.bincount(e_flat, length=N_EXPERTS)
    starts = jnp.cumsum(counts) - counts
    padded = (counts + GROUP_ROWS - 1) // GROUP_ROWS * GROUP_ROWS
    pends = jnp.cumsum(padded)
    pstarts = pends - padded
    dest = pstarts[e_s] + (jnp.arange(n_as) - starts[e_s])
    n_blk = -(-(n_as + N_EXPERTS * (GROUP_ROWS - 1)) // GROUP_ROWS)
    n_rows = n_blk * GROUP_ROWS
    buf = jnp.zeros((n_rows, D), x.dtype).at[dest].set(xt[tok_s])
    blk_e = jnp.minimum(jnp.sum(pends[None, :] <= (jnp.arange(n_blk) * GROUP_ROWS)[:, None], axis=1), N_EXPERTS - 1)

    def expert_block(args):
        xb, e = args
        h = xb @ w_gu[e] + b_gu[e]
        gate = jnp.minimum(h[:, :D_FF], SWIGLU_LIMIT)
        lin = jnp.clip(h[:, D_FF:], -SWIGLU_LIMIT, SWIGLU_LIMIT)
        glu = gate * jax.nn.sigmoid(SWIGLU_ALPHA * gate)
        return (glu * (lin + 1.0)) @ w_dn[e] + b_dn[e]

    y_buf = lax.map(expert_block, (buf.reshape(n_blk, GROUP_ROWS, D), blk_e)).reshape(n_rows, D)
    contrib = y_buf[dest] * g_s[:, None].astype(x.dtype)
    return jax.ops.segment_sum(contrib, tok_s, num_segments=n_tok).reshape(B, T, D)


def per_layer_embed(x, p, w_proj, w_gate):
    return x + jax.nn.sigmoid(x @ w_gate) * (p @ w_proj)


def setup_inputs(seed: int = 0) -> dict:
    key = jax.random.key(seed)
    ks = jax.random.split(key, 40)
    f32 = jnp.float32

    def nrm(k, shape, scale):
        return jax.random.normal(k, shape, f32) * scale

    n_pages = PAST_LEN // PAGE_SIZE
    n_used = DEC_BATCH * n_pages
    n_phys = n_used + max(1, n_used // 4)
    w_buf = min(WINDOW, PAST_LEN)
    page_table = jax.random.permutation(ks[0], n_phys)[:n_used].reshape(DEC_BATCH, n_pages).astype(jnp.int32)
    a0 = jax.random.uniform(ks[1], (N_RG, D_RNN), f32, 0.9, 0.999)
    sa = a0 ** (1.0 / RG_C)
    lam = jnp.log(sa) - jnp.log1p(-sa)
    return {
        'x_prompt': nrm(ks[2], (BATCH, SEQ, D_MODEL), 1.0),
        'x_sample': nrm(ks[3], (DEC_BATCH, DEC_SEQ, D_MODEL), 1.0),
        'cache_kv': nrm(ks[4], (N_NSA, n_phys, PAGE_SIZE, 4, KV_GROUPS, HEAD_DIM), 1.0),
        'state_win': nrm(ks[5], (N_NSA, DEC_BATCH, w_buf, 2, KV_GROUPS, HEAD_DIM), 1.0),
        'state_conv': nrm(ks[6], (N_RG, DEC_BATCH, CONV_W - 1, D_RNN), 1.0),
        'state_h': nrm(ks[7], (N_RG, DEC_BATCH, D_RNN), 0.5),
        'page_table': page_table,
        'p_prompt': nrm(ks[8], (DEPTH, BATCH, SEQ, PLE_DIM), 1.0),
        'p_sample': nrm(ks[9], (DEPTH, DEC_BATCH, DEC_SEQ, PLE_DIM), 1.0),
        'ln_g': 1.0 + nrm(ks[10], (DEPTH, 2, D_MODEL), 0.02),
        'ln_b': nrm(ks[11], (DEPTH, 2, D_MODEL), 0.02),
        'rg_w_in': nrm(ks[12], (N_RG, D_MODEL, 2 * D_RNN), D_MODEL ** -0.5),
        'rg_conv_w': nrm(ks[13], (N_RG, CONV_W, D_RNN), CONV_W ** -0.5),
        'rg_conv_b': nrm(ks[14], (N_RG, D_RNN), 0.02),
        'rg_w_a': nrm(ks[15], (N_RG, RG_BLOCKS, RG_BD, RG_BD), RG_BD ** -0.5),
        'rg_b_a': nrm(ks[16], (N_RG, D_RNN), 0.02),
        'rg_w_i': nrm(ks[17], (N_RG, RG_BLOCKS, RG_BD, RG_BD), RG_BD ** -0.5),
        'rg_b_i': nrm(ks[18], (N_RG, D_RNN), 0.02),
        'rg_lam': lam,
        'rg_w_o': nrm(ks[19], (N_RG, D_RNN, D_MODEL), DN_BETA * D_RNN ** -0.5),
        'nsa_w_in': nrm(ks[20], (N_NSA, D_MODEL, NSA_PROJ), D_MODEL ** -0.5),
        'nsa_pe_k': nrm(ks[21], (N_NSA, CMP_BLOCK, HEAD_DIM), 0.1),
        'nsa_ck_w1': nrm(ks[22], (N_NSA, CMP_BLOCK * HEAD_DIM, CMP_HIDDEN), (CMP_BLOCK * HEAD_DIM) ** -0.5),
        'nsa_ck_w2': nrm(ks[23], (N_NSA, CMP_HIDDEN, HEAD_DIM), CMP_HIDDEN ** -0.5),
        'nsa_pe_v': nrm(ks[24], (N_NSA, CMP_BLOCK, HEAD_DIM), 0.1),
        'nsa_cv_w1': nrm(ks[25], (N_NSA, CMP_BLOCK * HEAD_DIM, CMP_HIDDEN), (CMP_BLOCK * HEAD_DIM) ** -0.5),
        'nsa_cv_w2': nrm(ks[26], (N_NSA, CMP_HIDDEN, HEAD_DIM), CMP_HIDDEN ** -0.5),
        'nsa_w_o': nrm(ks[27], (N_NSA, NSA_Q, D_MODEL), DN_BETA * NSA_Q ** -0.5),
        'moe_w_r': nrm(ks[28], (DEPTH, D_MODEL, N_EXPERTS), D_MODEL ** -0.5),
        'moe_b_r': nrm(ks[29], (DEPTH, N_EXPERTS), 0.01),
        'moe_w_gu': nrm(ks[30], (DEPTH, N_EXPERTS, D_MODEL, 2 * D_FF), D_MODEL ** -0.5),
        'moe_b_gu': nrm(ks[31], (DEPTH, N_EXPERTS, 2 * D_FF), 0.01),
        'moe_w_dn': nrm(ks[32], (DEPTH, N_EXPERTS, D_FF, D_MODEL), DN_BETA * D_FF ** -0.5),
        'moe_b_dn': nrm(ks[33], (DEPTH, N_EXPERTS, D_MODEL), 0.01),
        'ple_w_proj': nrm(ks[34], (DEPTH, PLE_DIM, D_MODEL), PLE_DIM ** -0.5),
        'ple_w_gate': nrm(ks[35], (DEPTH, D_MODEL, D_MODEL), D_MODEL ** -0.5),
    }


def reference(x_prompt, x_sample, cache_kv, state_win, state_conv, state_h, page_table, p_prompt, p_sample,
              ln_g, ln_b, rg_w_in, rg_conv_w, rg_conv_b, rg_w_a, rg_b_a, rg_w_i, rg_b_i, rg_lam, rg_w_o,
              nsa_w_in, nsa_pe_k, nsa_ck_w1, nsa_ck_w2, nsa_pe_v, nsa_cv_w1, nsa_cv_w2, nsa_w_o,
              moe_w_r, moe_b_r, moe_w_gu, moe_b_gu, moe_w_dn, moe_b_dn, ple_w_proj, ple_w_gate):
    xp, xs = x_prompt, x_sample
    B = xp.shape[0]
    kv_p, kv_s, win_p, win_s, conv_p, conv_s, h_p, h_s = [], [], [], [], [], [], [], []
    for i in range(DEPTH):
        li = i // N_MIXERS
        if i % N_MIXERS == 0:
            mp, cp, hp = rglru_mixer(xp, jnp.zeros((B, CONV_W - 1, D_RNN), xp.dtype), jnp.zeros((B, D_RNN), jnp.float32),
                                     rg_w_in[li], rg_conv_w[li], rg_conv_b[li], rg_w_a[li], rg_b_a[li],
                                     rg_w_i[li], rg_b_i[li], rg_lam[li], rg_w_o[li])
            ms, cs, hs = rglru_mixer(xs, state_conv[li], state_h[li],
                                     rg_w_in[li], rg_conv_w[li], rg_conv_b[li], rg_w_a[li], rg_b_a[li],
                                     rg_w_i[li], rg_b_i[li], rg_lam[li], rg_w_o[li])
            conv_p.append(cp)
            conv_s.append(cs)
            h_p.append(hp)
            h_s.append(hs)
        else:
            mp, kp, wp = nsa_prompt(xp, nsa_w_in[li], nsa_pe_k[li], nsa_ck_w1[li], nsa_ck_w2[li],
                                    nsa_pe_v[li], nsa_cv_w1[li], nsa_cv_w2[li], nsa_w_o[li])
            ms, kss, wss = nsa_sample(xs, cache_kv, li, page_table, state_win[li], nsa_w_in[li], nsa_pe_k[li],
                                      nsa_ck_w1[li], nsa_ck_w2[li], nsa_pe_v[li], nsa_cv_w1[li], nsa_cv_w2[li], nsa_w_o[li])
            kv_p.append(kp)
            kv_s.append(kss)
            win_p.append(wp)
            win_s.append(wss)
        xp = layer_norm(DN_ALPHA * xp + mp, ln_g[i, 0], ln_b[i, 0])
        xs = layer_norm(DN_ALPHA * xs + ms, ln_g[i, 0], ln_b[i, 0])
        xp = layer_norm(DN_ALPHA * xp + moe_ffn(xp, moe_w_r[i], moe_b_r[i], moe_w_gu[i], moe_b_gu[i], moe_w_dn[i], moe_b_dn[i]),
                        ln_g[i, 1], ln_b[i, 1])
        xs = layer_norm(DN_ALPHA * xs + moe_ffn(xs, moe_w_r[i], moe_b_r[i], moe_w_gu[i], moe_b_gu[i], moe_w_dn[i], moe_b_dn[i]),
                        ln_g[i, 1], ln_b[i, 1])
        xp = per_layer_embed(xp, p_prompt[i], ple_w_proj[i], ple_w_gate[i])
        xs = per_layer_embed(xs, p_sample[i], ple_w_proj[i], ple_w_gate[i])
    return (xp, xs, jnp.stack(kv_p), jnp.stack(kv_s), jnp.stack(win_p), jnp.stack(win_s),
            jnp.stack(conv_p), jnp.stack(conv_s), jnp.stack(h_p), jnp.stack(h_s))
```

```python
import functools
import math

import jax
import jax.numpy as jnp
from jax import lax
from jax.experimental import pallas as pl
from jax.experimental.pallas import tpu as pltpu

F32 = jnp.float32
BF16 = jnp.bfloat16

D = 1024
N_HEADS = 16
HEAD_DIM = 64
KV_GROUPS = 4
GROUP_SIZE = 4
GD = KV_GROUPS * HEAD_DIM
ROT_HALF = 8
ROPE_THETA = 500000.0
CMP_BLOCK = 32
CMP_STRIDE = 16
CMP_HIDDEN = 128
SEL_BLOCK = 64
SEL_TOP = 16
WINDOW = 512
N_EXPERTS = 32
TOP_K = 4
SWIGLU_LIMIT = 7.0
SWIGLU_ALPHA = 1.702
PLE_DIM = 256
PAGE = 128
RG_C = 8.0
LN_EPS = 1e-5
NEG = -1e30
BIG = 1e30
NSA_PROJ = 2608
NSA_PROJ_PAD = 2688
LANES = 128
MOE_ROWS = 128
VMEM_LIMIT = 56 * 1024 * 1024


def _cparams(n_axes):
    return pltpu.CompilerParams(dimension_semantics=("arbitrary",) * n_axes,
                                vmem_limit_bytes=VMEM_LIMIT)


def _bdot(a, b):
    return jnp.dot(a, b, preferred_element_type=F32)


def _bdot_nt(a, b):
    return lax.dot_general(a, b, (((1,), (1,)), ((), ())), preferred_element_type=F32)


def _gelu(x):
    return 0.5 * x * (1.0 + jnp.tanh(math.sqrt(2.0 / math.pi) * (x + 0.044715 * (x * x * x))))


def _sigmoid(x):
    return 1.0 / (1.0 + jnp.exp(-x))


def _layer_norm(y, g, b):
    mu = jnp.mean(y, axis=-1, keepdims=True)
    yc = y - mu
    var = jnp.mean(yc * yc, axis=-1, keepdims=True)
    return yc * lax.rsqrt(var + LN_EPS) * g + b


def _lane_iota(shape):
    return lax.broadcasted_iota(jnp.int32, shape, len(shape) - 1)


def _row_iota(shape):
    return lax.broadcasted_iota(jnp.int32, shape, 0)


def _top_select(v, n_pick):
    lane = _lane_iota(v.shape).astype(F32)
    sel = jnp.zeros(v.shape, jnp.bool_)
    vals, idxs = [], []
    for _ in range(n_pick):
        m = jnp.max(v, axis=-1, keepdims=True)
        j = jnp.min(jnp.where(v == m, lane, float(v.shape[-1])), axis=-1, keepdims=True)
        pick = lane == j
        sel = jnp.logical_or(sel, pick)
        v = jnp.where(pick, -jnp.inf, v)
        vals.append(m)
        idxs.append(j)
    return sel, vals, idxs


def _mix_ln_router_kernel(alpha, x0_ref, m_ref, wo_ref, g_ref, b_ref, wr_ref, br_ref,
                          x1_ref, xp_ref, route_ref, wo_s, wrh_s, wrl_s):
    @pl.when(pl.program_id(0) == 0)
    def _():
        wo_s[...] = wo_ref[...].astype(BF16)
        wr = wr_ref[...]
        wrh = wr.astype(BF16)
        wrh_s[...] = wrh
        wrl_s[...] = (wr - wrh.astype(F32)).astype(BF16)

    mix = _bdot(m_ref[...].astype(BF16), wo_s[...])
    x1 = _layer_norm(alpha * x0_ref[...] + mix, g_ref[...], b_ref[...])
    x1_ref[...] = x1
    xh = x1.astype(BF16)
    xhf = xh.astype(F32)
    bits = lax.bitcast_convert_type(xhf, jnp.uint32)
    xp_ref[...] = jnp.bitwise_or(bits[:, D // 2:], lax.shift_right_logical(bits[:, :D // 2], jnp.uint32(16)))
    xl = (x1 - xhf).astype(BF16)
    logits = _bdot(xh, wrh_s[...]) + _bdot(xh, wrl_s[...]) + _bdot(xl, wrh_s[...]) + br_ref[...]
    _, vals, idxs = _top_select(logits, TOP_K)
    es = [jnp.exp(v - vals[0]) for v in vals]
    den = es[0] + es[1] + es[2] + es[3]
    lane = _lane_iota(logits.shape)
    route = jnp.zeros(logits.shape, F32)
    for k in range(TOP_K):
        route = jnp.where(lane == k, idxs[k], route)
        route = jnp.where(lane == TOP_K + k, es[k] / den, route)
    route_ref[...] = route


def mix_ln_router(x0, m, w_o, ln_g, ln_b, w_r, b_r, alpha, tm=512):
    n = x0.shape[0]
    wr = jnp.pad(w_r, ((0, 0), (0, LANES - N_EXPERTS)))
    br = jnp.pad(b_r, (0, LANES - N_EXPERTS), constant_values=NEG).reshape(1, LANES)
    row = lambda w: pl.BlockSpec((tm, w), lambda i: (i, 0))
    full = lambda a: pl.BlockSpec(a.shape, lambda i: (0,) * a.ndim)
    g2, b2 = ln_g.reshape(1, D), ln_b.reshape(1, D)
    return pl.pallas_call(
        functools.partial(_mix_ln_router_kernel, alpha),
        grid=(n // tm,),
        in_specs=[row(D), row(D), full(w_o), full(g2), full(b2), full(wr), full(br)],
        out_specs=[row(D), row(D // 2), row(LANES)],
        out_shape=[jax.ShapeDtypeStruct((n, D), F32), jax.ShapeDtypeStruct((n, D // 2), jnp.uint32),
                   jax.ShapeDtypeStruct((n, LANES), F32)],
        scratch_shapes=[pltpu.VMEM((D, D), BF16), pltpu.VMEM((D, LANES), BF16), pltpu.VMEM((D, LANES), BF16)],
        compiler_params=_cparams(1), name="mix_ln_router",
    )(x0, m, w_o, g2, b2, wr, br)


def _moe_kernel(start_ref, count_ref, tok_ref, gate_ref, xp_ref, wgu_ref, bgu_ref, wdn_ref, bdn_ref,
                out_ref, xg_s, y_s):
    c, e = pl.program_id(0), pl.program_id(1)

    @pl.when(jnp.logical_and(c == 0, e == 0))
    def _():
        xg_s[...] = jnp.zeros(xg_s.shape, xg_s.dtype)

    @pl.when(e == 0)
    def _():
        out_ref[...] = jnp.zeros(out_ref.shape, out_ref.dtype)

    start = start_ref[c, e]
    count = count_ref[c, e]

    def block(blk, carry):
        base = start + blk * MOE_ROWS
        rows = jnp.minimum(MOE_ROWS, count - blk * MOE_ROWS)

        def gather(i, cc):
            t = tok_ref[0, base + i]
            xg_s[pl.ds(i, 1), :] = xp_ref[pl.ds(t, 1), :]
            return cc

        lax.fori_loop(0, rows, gather, 0)
        u = xg_s[...]
        lo = lax.bitcast_convert_type(lax.shift_left(u, jnp.uint32(16)), F32)
        hi = lax.bitcast_convert_type(jnp.bitwise_and(u, jnp.uint32(0xFFFF0000)), F32)
        xb = jnp.concatenate([lo, hi], axis=1).astype(BF16)
        h = _bdot(xb, wgu_ref[...]) + bgu_ref[...]
        gate = jnp.minimum(h[:, :D], SWIGLU_LIMIT)
        lin = jnp.clip(h[:, D:], -SWIGLU_LIMIT, SWIGLU_LIMIT)
        act = gate * _sigmoid(SWIGLU_ALPHA * gate) * (lin + 1.0)
        y_s[...] = _bdot(act.astype(BF16), wdn_ref[...]) + bdn_ref[...]

        def scatter(i, cc):
            t = tok_ref[0, base + i]
            g = gate_ref[0, base + i]
            out_ref[pl.ds(t, 1), :] = out_ref[pl.ds(t, 1), :] + g * y_s[pl.ds(i, 1), :]
            return cc

        lax.fori_loop(0, rows, scatter, 0)
        return carry

    lax.fori_loop(0, (count + MOE_ROWS - 1) // MOE_ROWS, block, 0)


def moe_ffn(xp, route, w_gu16, b_gu, w_dn16, b_dn, n_chunks):
    n = xp.shape[0]
    ch = n // n_chunks
    n_as = ch * TOP_K
    e_ids = route[:, :TOP_K].astype(jnp.int32).reshape(n_chunks, n_as)
    gates = route[:, TOP_K:2 * TOP_K].reshape(n_chunks, n_as)
    tok = jnp.tile(jnp.arange(ch, dtype=jnp.int32)[:, None], (1, TOP_K)).reshape(1, n_as)
    order = jnp.argsort(e_ids, axis=1, stable=True)
    tok_s = jnp.take_along_axis(jnp.broadcast_to(tok, e_ids.shape), order, axis=1)
    gate_s = jnp.take_along_axis(gates, order, axis=1)
    counts = jnp.sum(e_ids[:, :, None] == jnp.arange(N_EXPERTS, dtype=jnp.int32)[None, None, :], axis=1).astype(jnp.int32)
    starts = (jnp.cumsum(counts, axis=1) - counts).astype(jnp.int32)
    tok_s = tok_s.reshape(n_chunks, 1, n_as)
    gate_s = gate_s.reshape(n_chunks, 1, n_as)
    grid_spec = pltpu.PrefetchScalarGridSpec(
        num_scalar_prefetch=2,
        grid=(n_chunks, N_EXPERTS),
        in_specs=[
            pl.BlockSpec((None, 1, n_as), lambda c, e, s, k: (c, 0, 0), memory_space=pltpu.SMEM),
            pl.BlockSpec((None, 1, n_as), lambda c, e, s, k: (c, 0, 0), memory_space=pltpu.SMEM),
            pl.BlockSpec((ch, D // 2), lambda c, e, s, k: (c, 0)),
            pl.BlockSpec((None, D, 2 * D), lambda c, e, s, k: (e, 0, 0)),
            pl.BlockSpec((None, 1, 2 * D), lambda c, e, s, k: (e, 0, 0)),
            pl.BlockSpec((None, D, D), lambda c, e, s, k: (e, 0, 0)),
            pl.BlockSpec((None, 1, D), lambda c, e, s, k: (e, 0, 0)),
        ],
        out_specs=pl.BlockSpec((ch, D), lambda c, e, s, k: (c, 0)),
        scratch_shapes=[pltpu.VMEM((MOE_ROWS, D // 2), jnp.uint32), pltpu.VMEM((MOE_ROWS, D), F32)],
    )
    return pl.pallas_call(
        _moe_kernel, grid_spec=grid_spec,
        out_shape=jax.ShapeDtypeStruct((n, D), F32),
        compiler_params=_cparams(2), name="moe",
    )(starts, counts, tok_s, gate_s, xp, w_gu16, b_gu.reshape(N_EXPERTS, 1, 2 * D), w_dn16,
      b_dn.reshape(N_EXPERTS, 1, D))


def _ln_ple_kernel(alpha, x1_ref, moe_ref, p_ref, g_ref, b_ref, wg_ref, wp_ref, o_ref, wg_s, wp_s):
    @pl.when(pl.program_id(0) == 0)
    def _():
        wg_s[...] = wg_ref[...].astype(BF16)
        wp_s[...] = wp_ref[...].astype(BF16)

    x2 = _layer_norm(alpha * x1_ref[...] + moe_ref[...], g_ref[...], b_ref[...])
    gate = _sigmoid(_bdot(x2.astype(BF16), wg_s[...]))
    o_ref[...] = x2 + gate * _bdot(p_ref[...].astype(BF16), wp_s[...])


def ln_ple(x1, moe, p, ln_g, ln_b, w_gate, w_proj, alpha, tm=512):
    n = x1.shape[0]
    row = lambda w: pl.BlockSpec((tm, w), lambda i: (i, 0))
    full = lambda a: pl.BlockSpec(a.shape, lambda i: (0,) * a.ndim)
    g2, b2 = ln_g.reshape(1, D), ln_b.reshape(1, D)
    return pl.pallas_call(
        functools.partial(_ln_ple_kernel, alpha),
        grid=(n // tm,),
        in_specs=[row(D), row(D), row(PLE_DIM), full(g2), full(b2), full(w_gate), full(w_proj)],
        out_specs=row(D),
        out_shape=jax.ShapeDtypeStruct((n, D), F32),
        scratch_shapes=[pltpu.VMEM((D, D), BF16), pltpu.VMEM((PLE_DIM, D), BF16)],
        compiler_params=_cparams(1), name="ln_ple",
    )(x1, moe, p, g2, b2, w_gate, w_proj)


def _rg_core(seg, x, win_s, cw_ref, cb_ref, wa_s, ba_ref, wi_s, bi_ref, lam_ref, shifted_u, h_prev):
    tm = x.shape[0]
    proj = _bdot(x.astype(BF16), win_s[...])
    y = _gelu(proj[:, :D])
    u = proj[:, D:]
    cw = cw_ref[...]
    conv = cb_ref[...] + cw[3:4] * u
    for k in (1, 2, 3):
        conv = conv + cw[3 - k:4 - k] * shifted_u(u, k)
    nb = wa_s.shape[0]
    bd = D // nb
    ra, ri = [], []
    for n in range(nb):
        cbn = conv[:, n * bd:(n + 1) * bd].astype(BF16)
        ra.append(_bdot(cbn, wa_s[n]))
        ri.append(_bdot(cbn, wi_s[n]))
    r = _sigmoid(jnp.concatenate(ra, axis=1) + ba_ref[...])
    ig = _sigmoid(jnp.concatenate(ri, axis=1) + bi_ref[...])
    z = -lam_ref[...]
    softplus = jnp.maximum(z, 0.0) + jnp.log1p(jnp.exp(-jnp.abs(z)))
    log_a = (-RG_C) * r * softplus
    a = jnp.exp(log_a)
    b = jnp.sqrt(-jnp.tanh(log_a) * (a * a + 1.0)) * ig * conv
    row = _row_iota((tm, D))
    pos = row if seg is None else jnp.bitwise_and(row, seg - 1)
    s = 1
    while s < (tm if seg is None else seg):
        keep = pos >= s
        a_sh = jnp.where(keep, pltpu.roll(a, s, 0), 1.0)
        b_sh = jnp.where(keep, pltpu.roll(b, s, 0), 0.0)
        b = a * b_sh + b
        a = a * a_sh
        s *= 2
    h = b + a * h_prev
    return u, h, (h * y).astype(BF16)


def _rg_cast_weights(win_ref, wa_ref, wi_ref, win_s, wa_s, wi_s):
    win_s[...] = win_ref[...].astype(BF16)
    wa_s[...] = wa_ref[...].astype(BF16)
    wi_s[...] = wi_ref[...].astype(BF16)


def _rg_prompt_kernel(x_ref, win_ref, cw_ref, cb_ref, wa_ref, ba_ref, wi_ref, bi_ref, lam_ref,
                      hy_ref, ul_ref, hl_ref, win_s, wa_s, wi_s, ucar_s, hcar_s):
    b, ti = pl.program_id(0), pl.program_id(1)

    @pl.when(jnp.logical_and(b == 0, ti == 0))
    def _():
        _rg_cast_weights(win_ref, wa_ref, wi_ref, win_s, wa_s, wi_s)

    @pl.when(ti == 0)
    def _():
        ucar_s[...] = jnp.zeros(ucar_s.shape, F32)
        hcar_s[...] = jnp.zeros(hcar_s.shape, F32)

    ucar = ucar_s[...]
    row8 = _row_iota((8, D))

    def shifted_u(u, k):
        rolled = pltpu.roll(u, k, 0)
        head = jnp.where(row8 < k, pltpu.roll(ucar, k, 0), rolled[:8])
        return jnp.concatenate([head, rolled[8:]], axis=0)

    u, h, hy = _rg_core(None, x_ref[...], win_s, cw_ref, cb_ref, wa_s, ba_ref, wi_s, bi_ref, lam_ref,
                        shifted_u, hcar_s[7:8, :])
    hy_ref[...] = hy
    tm = u.shape[0]
    ucar_s[...] = u[tm - 8:]
    hcar_s[...] = h[tm - 8:]
    ul_ref[...] = u[tm - 8:]
    hl_ref[...] = h[tm - 8:]


def _rg_sample_kernel(seg, x_ref, p1_ref, p2_ref, p3_ref, h0_ref, win_ref, cw_ref, cb_ref, wa_ref, ba_ref,
                      wi_ref, bi_ref, lam_ref, hy_ref, u_ref, h_ref, win_s, wa_s, wi_s):
    @pl.when(pl.program_id(0) == 0)
    def _():
        _rg_cast_weights(win_ref, wa_ref, wi_ref, win_s, wa_s, wi_s)

    prevs = (None, p1_ref, p2_ref, p3_ref)
    pos = jnp.bitwise_and(_row_iota(x_ref.shape), seg - 1)

    def shifted_u(u, k):
        return jnp.where(pos >= k, pltpu.roll(u, k, 0), prevs[k][...])

    u, h, hy = _rg_core(seg, x_ref[...], win_s, cw_ref, cb_ref, wa_s, ba_ref, wi_s, bi_ref, lam_ref,
                        shifted_u, h0_ref[...])
    hy_ref[...] = hy
    u_ref[...] = u
    h_ref[...] = h


def _rg_weight_args(w_in, conv_w, conv_b, w_a, b_a, w_i, b_i, lam):
    return (w_in, conv_w, conv_b.reshape(1, D), w_a, b_a.reshape(1, D), w_i, b_i.reshape(1, D), lam.reshape(1, D))


def _rg_scratch(w_a):
    return [pltpu.VMEM((D, 2 * D), BF16), pltpu.VMEM(w_a.shape, BF16), pltpu.VMEM(w_a.shape, BF16)]


def rg_prompt(x, batch, seq, weights, tm=256):
    wargs = _rg_weight_args(*weights)
    nt = seq // tm
    full = lambda a: pl.BlockSpec(a.shape, lambda b, t: (0,) * a.ndim)
    tail = pl.BlockSpec((None, 8, D), lambda b, t: (b, 0, 0))
    hy, ul, hl = pl.pallas_call(
        _rg_prompt_kernel, grid=(batch, nt),
        in_specs=[pl.BlockSpec((tm, D), lambda b, t: (b * nt + t, 0))] + [full(a) for a in wargs],
        out_specs=[pl.BlockSpec((tm, D), lambda b, t: (b * nt + t, 0)), tail, tail],
        out_shape=[jax.ShapeDtypeStruct((batch * seq, D), BF16), jax.ShapeDtypeStruct((batch, 8, D), F32),
                   jax.ShapeDtypeStruct((batch, 8, D), F32)],
        scratch_shapes=_rg_scratch(weights[3]) + [pltpu.VMEM((8, D), F32), pltpu.VMEM((8, D), F32)],
        compiler_params=_cparams(2), name="rg_prompt",
    )(x, *wargs)
    return hy, ul[:, 5:], hl[:, 7]


def rg_sample(x, n_seq, seg, conv_state, h_state, weights, tm=256):
    wargs = _rg_weight_args(*weights)
    n = n_seq * seg
    tm = min(tm, n)
    prevs = []
    for k in (1, 2, 3):
        pk = jnp.zeros((n_seq, seg, D), F32).at[:, :k].set(conv_state[:, 3 - k:])
        prevs.append(pk.reshape(n, D))
    h0 = jnp.broadcast_to(h_state[:, None, :], (n_seq, seg, D)).reshape(n, D)
    row = pl.BlockSpec((tm, D), lambda i: (i, 0))
    full = lambda a: pl.BlockSpec(a.shape, lambda i: (0,) * a.ndim)
    hy, u, h = pl.pallas_call(
        functools.partial(_rg_sample_kernel, seg), grid=(n // tm,),
        in_specs=[row] * 5 + [full(a) for a in wargs],
        out_specs=[row, row, row],
        out_shape=[jax.ShapeDtypeStruct((n, D), BF16), jax.ShapeDtypeStruct((n, D), F32),
                   jax.ShapeDtypeStruct((n, D), F32)],
        scratch_shapes=_rg_scratch(weights[3]),
        compiler_params=_cparams(1), name="rg_sample",
    )(x, *prevs, h0, *wargs)
    u = u.reshape(n_seq, seg, D)
    return hy, u[:, seg - 3:], h.reshape(n_seq, seg, D)[:, seg - 1]


def _rope_tables(pos):
    inv = ROPE_THETA ** (-jnp.arange(ROT_HALF, dtype=F32) / ROT_HALF)
    ang = pos.astype(F32)[:, None] * inv[None, :]
    cos, sin = jnp.cos(ang), jnp.sin(ang)
    n = pos.shape[0]
    rest = HEAD_DIM - 2 * ROT_HALF
    c = jnp.concatenate([cos, cos, jnp.ones((n, rest), F32)], axis=1)
    s1 = jnp.concatenate([-sin, jnp.zeros((n, HEAD_DIM - ROT_HALF), F32)], axis=1)
    s2 = jnp.concatenate([jnp.zeros((n, ROT_HALF), F32), sin, jnp.zeros((n, rest), F32)], axis=1)
    return tuple(jnp.tile(t, (1, LANES // HEAD_DIM)) for t in (c, s1, s2))


def _nsa_proj_kernel(x_ref, w_ref, ct_ref, s1_ref, s2_ref,
                     qx_ref, kv32_ref, win32_ref, kv16_ref, win16_ref, gate_ref, w_s):
    @pl.when(pl.program_id(0) == 0)
    def _():
        w_s[...] = w_ref[...].astype(BF16)

    proj = _bdot(x_ref[...].astype(BF16), w_s[...])
    ct, s1, s2 = ct_ref[...], s1_ref[...], s2_ref[...]

    def rope(blk):
        reps = blk.shape[1] // LANES
        w = blk.shape[1]
        return (blk * jnp.tile(ct, (1, reps)) + pltpu.roll(blk, w - ROT_HALF, 1) * jnp.tile(s1, (1, reps))
                + pltpu.roll(blk, ROT_HALF, 1) * jnp.tile(s2, (1, reps)))

    lane_grp = _lane_iota((x_ref.shape[0], GD)) // HEAD_DIM
    scale = HEAD_DIM ** -0.5
    for g in range(KV_GROUPS):
        qg = rope(proj[:, g * GD:(g + 1) * GD]) * scale
        for r in range(GROUP_SIZE):
            moved = qg if r == g else pltpu.roll(qg, ((g - r) % KV_GROUPS) * HEAD_DIM, 1)
            h = g * GROUP_SIZE + r
            qx_ref[:, h * GD:(h + 1) * GD] = jnp.where(lane_grp == g, moved, 0.0).astype(BF16)
    blocks = []
    for i in range(6):
        blk = proj[:, D + i * GD:D + (i + 1) * GD]
        blocks.append(rope(blk) if i % 2 == 0 else blk)
    kv = jnp.concatenate(blocks[:4], axis=1)
    win = jnp.concatenate(blocks[4:], axis=1)
    kv32_ref[...] = kv
    win32_ref[...] = win
    kv16_ref[...] = kv.astype(BF16)
    win16_ref[...] = win.astype(BF16)
    gate_ref[...] = _sigmoid(proj[:, D + 6 * GD:])


def nsa_proj(x, pos, w_in, tm=256):
    n = x.shape[0]
    w = jnp.pad(w_in, ((0, 0), (0, NSA_PROJ_PAD - NSA_PROJ)))
    tabs = _rope_tables(pos)
    row = lambda wd: pl.BlockSpec((tm, wd), lambda i: (i, 0))
    return pl.pallas_call(
        _nsa_proj_kernel, grid=(n // tm,),
        in_specs=[row(D), pl.BlockSpec(w.shape, lambda i: (0, 0)), row(LANES), row(LANES), row(LANES)],
        out_specs=[row(N_HEADS * GD), row(D), row(2 * GD), row(D), row(2 * GD), row(LANES)],
        out_shape=[jax.ShapeDtypeStruct((n, N_HEADS * GD), BF16), jax.ShapeDtypeStruct((n, D), F32),
                   jax.ShapeDtypeStruct((n, 2 * GD), F32), jax.ShapeDtypeStruct((n, D), BF16),
                   jax.ShapeDtypeStruct((n, 2 * GD), BF16), jax.ShapeDtypeStruct((n, LANES), F32)],
        scratch_shapes=[pltpu.VMEM((D, NSA_PROJ_PAD), BF16)],
        compiler_params=_cparams(1), name="nsa_proj",
    )(x, w, *tabs)


def _cmp_accumulate(acc_ref, m, x16, w1_ref, l):
    half = (CMP_BLOCK // 2) * HEAD_DIM
    off = l * HEAD_DIM if isinstance(l, int) else pl.multiple_of(l * HEAD_DIM, HEAD_DIM)
    top = w1_ref[pl.ds(off, HEAD_DIM), :]
    bot = w1_ref[pl.ds(half + off, HEAD_DIM), :]
    w = jnp.concatenate([top, bot], axis=1).astype(BF16)
    wrep = jnp.concatenate([w] * KV_GROUPS, axis=0)
    lane_grp = _lane_iota(x16.shape) // HEAD_DIM
    for g in range(KV_GROUPS):
        xg = jnp.where(lane_grp == g, x16, jnp.zeros_like(x16))
        acc_ref[m, g] = acc_ref[m, g] + _bdot(xg, wrep)


def _cmp_finalize(acc_ref, m, pe_ref, w1_ref, w2p_ref):
    nch = acc_ref.shape[2]
    pe8 = jnp.broadcast_to(pe_ref[...], (8, pe_ref.shape[1])).astype(BF16)
    bias = _bdot(pe8, w1_ref[...].astype(BF16))[0:1]
    out = jnp.zeros((nch, GD), F32)
    for g in range(KV_GROUPS):
        a = acc_ref[m, g]
        hid = a[:, :CMP_HIDDEN] + pltpu.roll(a[:, CMP_HIDDEN:], nch - 1, 0) + bias
        out = out + _bdot(_gelu(hid).astype(BF16), w2p_ref[g].astype(BF16))
    return out


def _cmp_prompt_kernel(kv_ref, w1k_ref, w1v_ref, pek_ref, pev_ref, w2k_ref, w2v_ref, kc_ref, vc_ref, acc_s):
    l = pl.program_id(1)

    @pl.when(l == 0)
    def _():
        acc_s[...] = jnp.zeros(acc_s.shape, F32)

    blk = kv_ref[...]
    _cmp_accumulate(acc_s, 0, blk[:, :GD], w1k_ref, l)
    _cmp_accumulate(acc_s, 1, blk[:, GD:2 * GD], w1v_ref, l)

    @pl.when(l == CMP_STRIDE - 1)
    def _():
        kc_ref[...] = _cmp_finalize(acc_s, 0, pek_ref, w1k_ref, w2k_ref).astype(BF16)
        vc_ref[...] = _cmp_finalize(acc_s, 1, pev_ref, w1v_ref, w2v_ref).astype(BF16)


def _w2_padded(w2):
    out = jnp.zeros((KV_GROUPS, CMP_HIDDEN, GD), F32)
    for g in range(KV_GROUPS):
        out = out.at[g, :, g * HEAD_DIM:(g + 1) * HEAD_DIM].set(w2)
    return out


def cmp_prompt(kv16, batch, seq, cmp_w):
    pe_k, w1k, w2k, pe_v, w1v, w2v = cmp_w
    nch = seq // CMP_STRIDE
    kvr = kv16.reshape(batch, nch, CMP_STRIDE * D)
    full = lambda a: pl.BlockSpec(a.shape, lambda b, l: (0,) * a.ndim)
    args = (w1k, w1v, pe_k.reshape(1, -1), pe_v.reshape(1, -1), _w2_padded(w2k), _w2_padded(w2v))
    out = pl.BlockSpec((None, nch, GD), lambda b, l: (b, 0, 0))
    return pl.pallas_call(
        _cmp_prompt_kernel, grid=(batch, CMP_STRIDE),
        in_specs=[pl.BlockSpec((None, nch, D), lambda b, l: (b, 0, l))] + [full(a) for a in args],
        out_specs=[out, out],
        out_shape=[jax.ShapeDtypeStruct((batch, nch, GD), BF16)] * 2,
        scratch_shapes=[pltpu.VMEM((2, KV_GROUPS, nch, 2 * CMP_HIDDEN), F32)],
        compiler_params=_cparams(2), name="cmp_prompt",
    )(kvr, *args)


def _overlap_matrix(n_c_rows, n_sel):
    c0 = jnp.arange(n_c_rows)[:, None] * CMP_STRIDE
    j0 = jnp.arange(LANES)[None, :] * SEL_BLOCK
    ov = (c0 < j0 + SEL_BLOCK) & (c0 + CMP_BLOCK > j0) & (jnp.arange(LANES)[None, :] < n_sel)
    return ov.astype(BF16)


def _masked_softmax(s, ok):
    s = jnp.where(ok, s, NEG)
    m = jnp.max(s, axis=-1, keepdims=True)
    e = jnp.where(ok, jnp.exp(s - m), 0.0)
    return e, jnp.sum(e, axis=-1, keepdims=True)


def _importance_select(p_grp, ov_ref, tq):
    ph = p_grp.astype(BF16)
    pl_ = (p_grp - ph.astype(F32)).astype(BF16)
    imp = _bdot(ph, ov_ref[...]) + _bdot(pl_, ov_ref[...])
    jb = _lane_iota(imp.shape)
    jq = tq // SEL_BLOCK
    forced = jnp.logical_or(jb == 0, jnp.logical_or(jb == jq, jb == jq - 1))
    imp = jnp.where(forced, BIG, imp)
    imp = jnp.where(jb > jq, -BIG, imp)
    sel, _, _ = _top_select(imp, SEL_TOP)
    return jnp.where(sel, 1.0, 0.0).astype(BF16)


def _attn_prompt_kernel(tk, qx_ref, kc_ref, vc_ref, ks_ref, vs_ref, kw_ref, vw_ref, gate_ref, ov_ref,
                        o_ref, m_s, l_s, acc_s):
    qi, g = pl.program_id(1), pl.program_id(2)
    tq_n = qx_ref.shape[0]
    rows = GROUP_SIZE * tq_n
    t0 = qi * tq_n
    qb = qx_ref[...]
    q = jnp.concatenate([qb[:, r * GD:(r + 1) * GD] for r in range(GROUP_SIZE)], axis=0)
    tq1 = t0 + _row_iota((tq_n, 1))
    tq = jnp.concatenate([tq1] * GROUP_SIZE, axis=0)

    s = _bdot_nt(q, kc_ref[...])
    cend = _lane_iota(s.shape) * CMP_STRIDE + (CMP_BLOCK - 1)
    e, den = _masked_softmax(s, cend <= tq)
    p_c = e / jnp.maximum(den, 1e-30)
    o_c = _bdot(p_c.astype(BF16), vc_ref[...])
    p_grp = p_c[0:tq_n]
    for r in range(1, GROUP_SIZE):
        p_grp = p_grp + p_c[r * tq_n:(r + 1) * tq_n]
    sel = _importance_select(p_grp, ov_ref, tq1)
    sel4 = jnp.concatenate([sel] * GROUP_SIZE, axis=0)

    m_s[...] = jnp.full(m_s.shape, NEG, F32)
    l_s[...] = jnp.zeros(l_s.shape, F32)
    acc_s[...] = jnp.zeros(acc_s.shape, F32)

    def key_tile(kt, carry):
        k0 = pl.multiple_of(kt * tk, tk)
        s = _bdot_nt(q, ks_ref[pl.ds(k0, tk), :])
        kpos = k0 + _lane_iota(s.shape)
        expand = (_row_iota((LANES, tk)) == (k0 + _lane_iota((LANES, tk))) // SEL_BLOCK)
        selx = _bdot(sel4, jnp.where(expand, 1.0, 0.0).astype(BF16))
        ok = jnp.logical_and(selx > 0.5, kpos <= tq)
        s = jnp.where(ok, s, NEG)
        m_old = m_s[...]
        m_new = jnp.maximum(m_old, jnp.max(s, axis=-1, keepdims=True))
        a = jnp.exp(m_old - m_new)
        p = jnp.where(ok, jnp.exp(s - m_new), 0.0)
        l_s[...] = a * l_s[...] + jnp.sum(p, axis=-1, keepdims=True)
        acc_s[...] = a * acc_s[...] + _bdot(p.astype(BF16), vs_ref[pl.ds(k0, tk), :])
        m_s[...] = m_new
        return carry

    lax.fori_loop(0, (t0 + tq_n + tk - 1) // tk, key_tile, 0)
    o_s = acc_s[...] / l_s[...]

    wk = WINDOW + tq_n
    w0 = pl.multiple_of(jnp.maximum(t0 - WINDOW, 0), tq_n)
    s = _bdot_nt(q, kw_ref[pl.ds(w0, wk), :])
    dist = tq - (w0 + _lane_iota(s.shape))
    e, den = _masked_softmax(s, jnp.logical_and(dist >= 0, dist <= WINDOW))
    o_w = _bdot((e / den).astype(BF16), vw_ref[pl.ds(w0, wk), :])

    gates = gate_ref[...]
    glane = _lane_iota(gates.shape)

    def gate_col(branch):
        cols = [jnp.sum(jnp.where(glane == (g * GROUP_SIZE + r) * 3 + branch, gates, 0.0), axis=-1, keepdims=True)
                for r in range(GROUP_SIZE)]
        return jnp.concatenate(cols, axis=0)

    o = gate_col(0) * o_c + gate_col(1) * o_s + gate_col(2) * o_w
    ocat = jnp.concatenate([o[r * tq_n:(r + 1) * tq_n] for r in range(GROUP_SIZE)], axis=1).astype(BF16)
    src = _row_iota((GROUP_SIZE * GD, GD))
    dst = _lane_iota((GROUP_SIZE * GD, GD))
    move = src == (dst // HEAD_DIM) * GD + g * HEAD_DIM + dst % HEAD_DIM
    o_ref[...] = _bdot(ocat, jnp.where(move, 1.0, 0.0).astype(BF16)).astype(o_ref.dtype)


def attn_prompt(qx, kc, vc, kv16, win16, gates, batch, seq, tq=128, tk=512):
    nq = seq // tq
    nch = seq // CMP_STRIDE
    n_sel = -(-seq // SEL_BLOCK)
    ov = _overlap_matrix(nch, n_sel)
    kv3 = kv16.reshape(batch, seq, D)
    win3 = win16.reshape(batch, seq, 2 * GD)
    slab = lambda col: pl.BlockSpec((None, seq, GD), lambda b, i, g: (b, 0, col))
    cmp_spec = pl.BlockSpec((None, nch, GD), lambda b, i, g: (b, 0, 0))
    rows = GROUP_SIZE * tq
    return pl.pallas_call(
        functools.partial(_attn_prompt_kernel, tk), grid=(batch, nq, KV_GROUPS),
        in_specs=[pl.BlockSpec((tq, GROUP_SIZE * GD), lambda b, i, g: (b * nq + i, g)),
                  cmp_spec, cmp_spec, slab(2), slab(3), slab(0), slab(1),
                  pl.BlockSpec((tq, LANES), lambda b, i, g: (b * nq + i, 0)),
                  pl.BlockSpec(ov.shape, lambda b, i, g: (0, 0))],
        out_specs=pl.BlockSpec((tq, GD), lambda b, i, g: (b * nq + i, g)),
        out_shape=jax.ShapeDtypeStruct((batch * seq, D), BF16),
        scratch_shapes=[pltpu.VMEM((rows, 1), F32), pltpu.VMEM((rows, 1), F32), pltpu.VMEM((rows, GD), F32)],
        compiler_params=_cparams(3), name="attn_prompt",
    )(qx, kc, vc, kv3, kv3, win3, win3, gates, ov)


def _page_slab(page_ref, slab, first_row, n_rows, row_stride):
    tiles = D // LANES
    per_slab = GD // LANES
    parts = [page_ref[pl.ds(first_row * tiles + slab * per_slab + t, n_rows, stride=row_stride * tiles), :]
             for t in range(per_slab)]
    return jnp.concatenate(parts, axis=1)


def _nsa_sample_kernel(n_pages, past, pt_ref, *refs):
    page_refs = refs[:n_pages]
    (st_ref, qx_ref, kvn_ref, winn_ref, gate_ref, w1k_ref, w1v_ref, pek_ref, pev_ref, w2k_ref, w2v_ref,
     ov_ref, move_ref, o_ref, nwin_ref, acc_s) = refs[n_pages:]
    t_new = qx_ref.shape[0]
    rows = N_HEADS * t_new
    wb = st_ref.shape[0]

    acc_s[...] = jnp.zeros(acc_s.shape, F32)
    per_page = PAGE // CMP_STRIDE
    for l in range(CMP_STRIDE):
        for m, w1_ref in ((0, w1k_ref), (1, w1v_ref)):
            x = jnp.concatenate([_page_slab(p, m, l, per_page, CMP_STRIDE) for p in page_refs], axis=0)
            _cmp_accumulate(acc_s, m, x.astype(BF16), w1_ref, l)
    kc = _cmp_finalize(acc_s, 0, pek_ref, w1k_ref, w2k_ref).astype(BF16)
    vc = _cmp_finalize(acc_s, 1, pev_ref, w1v_ref, w2v_ref).astype(BF16)

    qb = qx_ref[...]
    q = jnp.concatenate([qb[:, h * GD:(h + 1) * GD] for h in range(N_HEADS)], axis=0).astype(BF16)
    qidx = jnp.bitwise_and(_row_iota((rows, 1)), t_new - 1)
    tq = past + qidx

    s = _bdot_nt(q, kc)
    cend = _lane_iota(s.shape) * CMP_STRIDE + (CMP_BLOCK - 1)
    e, den = _masked_softmax(s, cend <= tq)
    p_c = e / jnp.maximum(den, 1e-30)
    o_c = _bdot(p_c.astype(BF16), vc)

    grp = []
    for g in range(KV_GROUPS):
        acc = p_c[g * GROUP_SIZE * t_new:(g * GROUP_SIZE + 1) * t_new]
        for r in range(1, GROUP_SIZE):
            h = g * GROUP_SIZE + r
            acc = acc + p_c[h * t_new:(h + 1) * t_new]
        grp.append(acc)
    p_grp = jnp.concatenate(grp, axis=0)
    sel = _importance_select(p_grp, ov_ref, past + jnp.bitwise_and(_row_iota((KV_GROUPS * t_new, 1)), t_new - 1))
    sel_h = jnp.concatenate([sel[(h // GROUP_SIZE) * t_new:(h // GROUP_SIZE + 1) * t_new] for h in range(N_HEADS)],
                            axis=0)

    def new_keys(x):
        return jnp.concatenate([x, jnp.zeros((LANES - t_new, x.shape[1]), x.dtype)], axis=0).astype(BF16)

    new_ok = jnp.logical_and(_lane_iota((rows, LANES)) < t_new, _lane_iota((rows, LANES)) <= qidx)

    k_all = jnp.concatenate([_page_slab(p, 2, 0, PAGE, 1) for p in page_refs], axis=0).astype(BF16)
    v_all = jnp.concatenate([_page_slab(p, 3, 0, PAGE, 1) for p in page_refs], axis=0).astype(BF16)
    kvn = kvn_ref[...]
    s_p = _bdot_nt(q, k_all)
    expand = _row_iota((LANES, past)) == _lane_iota((LANES, past)) // SEL_BLOCK
    ok_p = _bdot(sel_h, jnp.where(expand, 1.0, 0.0).astype(BF16)) > 0.5
    s_n = _bdot_nt(q, new_keys(kvn[:, 2 * GD:3 * GD]))
    s_p = jnp.where(ok_p, s_p, NEG)
    s_n = jnp.where(new_ok, s_n, NEG)
    m = jnp.maximum(jnp.max(s_p, axis=-1, keepdims=True), jnp.max(s_n, axis=-1, keepdims=True))
    e_p = jnp.where(ok_p, jnp.exp(s_p - m), 0.0)
    e_n = jnp.where(new_ok, jnp.exp(s_n - m), 0.0)
    den = jnp.sum(e_p, axis=-1, keepdims=True) + jnp.sum(e_n, axis=-1, keepdims=True)
    o_s = (_bdot(e_p.astype(BF16), v_all) + _bdot(e_n.astype(BF16), new_keys(kvn[:, 3 * GD:4 * GD]))) / den

    st = st_ref[...]
    winn = winn_ref[...]
    s_w = _bdot_nt(q, st[:, :GD].astype(BF16))
    dist = tq - (past - wb + _lane_iota(s_w.shape))
    ok_w = jnp.logical_and(dist >= 0, dist <= WINDOW)
    s_n = _bdot_nt(q, new_keys(winn[:, :GD]))
    s_w = jnp.where(ok_w, s_w, NEG)
    s_n = jnp.where(new_ok, s_n, NEG)
    m = jnp.maximum(jnp.max(s_w, axis=-1, keepdims=True), jnp.max(s_n, axis=-1, keepdims=True))
    e_w = jnp.where(ok_w, jnp.exp(s_w - m), 0.0)
    e_n = jnp.where(new_ok, jnp.exp(s_n - m), 0.0)
    den = jnp.sum(e_w, axis=-1, keepdims=True) + jnp.sum(e_n, axis=-1, keepdims=True)
    o_w = (_bdot(e_w.astype(BF16), st[:, GD:].astype(BF16)) + _bdot(e_n.astype(BF16), new_keys(winn[:, GD:]))) / den

    gates = gate_ref[...]

    def gate_col(branch):
        return jnp.concatenate([gates[:, h * 3 + branch:h * 3 + branch + 1] for h in range(N_HEADS)], axis=0)

    o = gate_col(0) * o_c + gate_col(1) * o_s + gate_col(2) * o_w
    outs = []
    for g in range(KV_GROUPS):
        ocat = jnp.concatenate([o[(g * GROUP_SIZE + r) * t_new:(g * GROUP_SIZE + r + 1) * t_new]
                                for r in range(GROUP_SIZE)], axis=1).astype(BF16)
        outs.append(_bdot(ocat, move_ref[g]))
    o_ref[...] = jnp.concatenate(outs, axis=1)

    nwin_ref[0:wb - t_new, :] = st[t_new:, :]
    nwin_ref[wb - t_new:wb, :] = winn


def nsa_sample(cache4, li, page_table, state_win3, qx_s, kvn, winn, gates_s, cmp_w, past):
    pe_k, w1k, w2k, pe_v, w1v, w2v = cmp_w
    n_seq, n_pages = page_table.shape
    t_new = qx_s.shape[0] // n_seq
    wb = state_win3.shape[2]
    nch = past // CMP_STRIDE
    n_sel = -(-(past + t_new) // SEL_BLOCK)
    ov = _overlap_matrix(nch, n_sel)
    src = jnp.arange(GROUP_SIZE * GD)[None, :, None]
    dst = jnp.arange(GD)[None, None, :]
    grp = jnp.arange(KV_GROUPS)[:, None, None]
    move = (src == (dst // HEAD_DIM) * GD + grp * HEAD_DIM + dst % HEAD_DIM).astype(BF16)
    consts = (w1k, w1v, pe_k.reshape(1, -1), pe_v.reshape(1, -1), _w2_padded(w2k), _w2_padded(w2v), ov, move)
    full = lambda a: pl.BlockSpec(a.shape, lambda b, pt: (0,) * a.ndim)
    seq_rows = lambda w: pl.BlockSpec((t_new, w), lambda b, pt: (b, 0))
    page_spec = lambda p: pl.BlockSpec((None, None, PAGE * D // LANES, LANES), lambda b, pt: (li, pt[b, p], 0, 0))
    grid_spec = pltpu.PrefetchScalarGridSpec(
        num_scalar_prefetch=1, grid=(n_seq,),
        in_specs=[page_spec(p) for p in range(n_pages)]
        + [pl.BlockSpec((None, None, wb, 2 * GD), lambda b, pt: (li, b, 0, 0)),
           seq_rows(N_HEADS * GD), seq_rows(D), seq_rows(2 * GD), seq_rows(LANES)]
        + [full(a) for a in consts],
        out_specs=[seq_rows(D), pl.BlockSpec((None, wb, 2 * GD), lambda b, pt: (b, 0, 0))],
        scratch_shapes=[pltpu.VMEM((2, KV_GROUPS, nch, 2 * CMP_HIDDEN), F32)],
    )
    return pl.pallas_call(
        functools.partial(_nsa_sample_kernel, n_pages, past), grid_spec=grid_spec,
        out_shape=[jax.ShapeDtypeStruct((n_seq * t_new, D), F32), jax.ShapeDtypeStruct((n_seq, wb, 2 * GD), F32)],
        compiler_params=_cparams(1), name="nsa_sample",
    )(page_table, *([cache4] * n_pages), state_win3, qx_s, kvn, winn, gates_s, *consts)


def _moe_chunks(n):
    chunks = 1
    while n // chunks > 2304 and (n // chunks) % 16 == 0:
        chunks *= 2
    return chunks


def kernel(x_prompt, x_sample, cache_kv, state_win, state_conv, state_h, page_table, p_prompt, p_sample, ln_g, ln_b, rg_w_in, rg_conv_w, rg_conv_b, rg_w_a, rg_b_a, rg_w_i, rg_b_i, rg_lam, rg_w_o, nsa_w_in, nsa_pe_k, nsa_ck_w1, nsa_ck_w2, nsa_pe_v, nsa_cv_w1, nsa_cv_w2, nsa_w_o, moe_w_r, moe_b_r, moe_w_gu, moe_b_gu, moe_w_dn, moe_b_dn, ple_w_proj, ple_w_gate):
    batch, seq = x_prompt.shape[:2]
    n_seq, t_new = x_sample.shape[:2]
    depth = ln_g.shape[0]
    past = page_table.shape[1] * PAGE
    wb = state_win.shape[2]
    assert seq >= wb and wb == min(WINDOW, past)
    alpha = (2 * depth) ** 0.25
    n_p, n_s = batch * seq, n_seq * t_new
    n = n_p + n_s
    x = jnp.concatenate([x_prompt.reshape(n_p, D), x_sample.reshape(n_s, D)], axis=0)
    p_all = jnp.concatenate([p_prompt.reshape(depth, n_p, PLE_DIM), p_sample.reshape(depth, n_s, PLE_DIM)], axis=1)
    pos = jnp.concatenate([jnp.tile(jnp.arange(seq), batch), past + jnp.tile(jnp.arange(t_new), n_seq)])
    cache4 = cache_kv.reshape(cache_kv.shape[0], cache_kv.shape[1], PAGE * D // LANES, LANES)
    win3 = state_win.reshape(state_win.shape[0], n_seq, wb, 2 * GD)
    w_gu16 = moe_w_gu.astype(BF16)
    w_dn16 = moe_w_dn.astype(BF16)
    n_chunks = _moe_chunks(n)
    kv_p, kv_s, win_p, win_s, conv_p, conv_s, h_p, h_s = [], [], [], [], [], [], [], []
    for i in range(depth):
        li = i // 2
        if i % 2 == 0:
            weights = (rg_w_in[li], rg_conv_w[li], rg_conv_b[li], rg_w_a[li], rg_b_a[li], rg_w_i[li], rg_b_i[li],
                       rg_lam[li])
            hy_p, cp, hp = rg_prompt(x[:n_p], batch, seq, weights)
            hy_s, cs, hs = rg_sample(x[n_p:], n_seq, t_new, state_conv[li], state_h[li], weights)
            conv_p.append(cp)
            conv_s.append(cs)
            h_p.append(hp)
            h_s.append(hs)
            mixed = jnp.concatenate([hy_p, hy_s], axis=0)
            w_o = rg_w_o[li]
        else:
            qx, kv32, win32, kv16, win16, gates = nsa_proj(x, pos, nsa_w_in[li])
            cmp_w = (nsa_pe_k[li], nsa_ck_w1[li], nsa_ck_w2[li], nsa_pe_v[li], nsa_cv_w1[li], nsa_cv_w2[li])
            kc, vc = cmp_prompt(kv16[:n_p], batch, seq, cmp_w)
            o_p = attn_prompt(qx[:n_p], kc, vc, kv16[:n_p], win16[:n_p], gates[:n_p], batch, seq)
            o_s, nwin = nsa_sample(cache4, li, page_table, win3, qx[n_p:].astype(F32), kv32[n_p:], win32[n_p:],
                                   gates[n_p:], cmp_w, past)
            kv_p.append(kv32[:n_p].reshape(batch, seq, 4, KV_GROUPS, HEAD_DIM))
            kv_s.append(kv32[n_p:].reshape(n_seq, t_new, 4, KV_GROUPS, HEAD_DIM))
            win_p.append(win32[:n_p].reshape(batch, seq, 2, KV_GROUPS, HEAD_DIM)[:, seq - wb:])
            win_s.append(nwin.reshape(n_seq, wb, 2, KV_GROUPS, HEAD_DIM))
            mixed = jnp.concatenate([o_p, o_s.astype(BF16)], axis=0)
            w_o = nsa_w_o[li]
        x1, xp, route = mix_ln_router(x, mixed, w_o, ln_g[i, 0], ln_b[i, 0], moe_w_r[i], moe_b_r[i], alpha)
        moe = moe_ffn(xp, route, w_gu16[i], moe_b_gu[i], w_dn16[i], moe_b_dn[i], n_chunks)
        x = ln_ple(x1, moe, p_all[i], ln_g[i, 1], ln_b[i, 1], ple_w_gate[i], ple_w_proj[i], alpha)
    return (x[:n_p].reshape(batch, seq, D), x[n_p:].reshape(n_seq, t_new, D),
            jnp.stack(kv_p), jnp.stack(kv_s), jnp.stack(win_p), jnp.stack(win_s),
            jnp.stack(conv_p), jnp.stack(conv_s), jnp.stack(h_p), jnp.stack(h_s))
```

```python
import functools
import math

import jax
import jax.numpy as jnp
from jax import lax
from jax.experimental import pallas as pl
from jax.experimental.pallas import tpu as pltpu

F32 = jnp.float32
BF16 = jnp.bfloat16

D = 1024
N_HEADS = 16
HEAD_DIM = 64
KV_GROUPS = 4
GROUP_SIZE = 4
GD = KV_GROUPS * HEAD_DIM
ROT_HALF = 8
ROPE_THETA = 500000.0
CMP_BLOCK = 32
CMP_STRIDE = 16
CMP_HIDDEN = 128
SEL_BLOCK = 64
SEL_TOP = 16
WINDOW = 512
N_EXPERTS = 32
TOP_K = 4
SWIGLU_LIMIT = 7.0
SWIGLU_ALPHA = 1.702
PLE_DIM = 256
PAGE = 128
RG_C = 8.0
LN_EPS = 1e-5
NEG = -1e30
BIG = 1e30
NSA_PROJ = 2608
NSA_PROJ_PAD = 2688
LANES = 128
MOE_ROWS = 144
VMEM_LIMIT = 56 * 1024 * 1024


def _cparams(n_axes):
    return pltpu.CompilerParams(dimension_semantics=("arbitrary",) * n_axes,
                                vmem_limit_bytes=VMEM_LIMIT)


def _bdot(a, b):
    return jnp.dot(a, b, preferred_element_type=F32)


def _bdot_nt(a, b):
    return lax.dot_general(a, b, (((1,), (1,)), ((), ())), preferred_element_type=F32)


def _gelu(x):
    return 0.5 * x * (1.0 + jnp.tanh(math.sqrt(2.0 / math.pi) * (x + 0.044715 * (x * x * x))))


def _sigmoid(x):
    return 1.0 / (1.0 + jnp.exp(-x))


def _layer_norm(y, g, b):
    mu = jnp.mean(y, axis=-1, keepdims=True)
    yc = y - mu
    var = jnp.mean(yc * yc, axis=-1, keepdims=True)
    return yc * lax.rsqrt(var + LN_EPS) * g + b


def _lane_iota(shape):
    return lax.broadcasted_iota(jnp.int32, shape, len(shape) - 1)


def _row_iota(shape):
    return lax.broadcasted_iota(jnp.int32, shape, 0)


def _top_select(v, n_pick):
    lane = _lane_iota(v.shape).astype(F32)
    sel = jnp.zeros(v.shape, jnp.bool_)
    vals, idxs = [], []
    for _ in range(n_pick):
        m = jnp.max(v, axis=-1, keepdims=True)
        j = jnp.min(jnp.where(v == m, lane, float(v.shape[-1])), axis=-1, keepdims=True)
        pick = lane == j
        sel = jnp.logical_or(sel, pick)
        v = jnp.where(pick, -jnp.inf, v)
        vals.append(m)
        idxs.append(j)
    return sel, vals, idxs


def _mix_ln_router_kernel(alpha, x0_ref, m_ref, wo_ref, g_ref, b_ref, wr_ref, br_ref,
                          x1_ref, xp_ref, route_ref, wo_s, wrh_s, wrl_s):
    @pl.when(pl.program_id(0) == 0)
    def _():
        wo_s[...] = wo_ref[...].astype(BF16)
        wr = wr_ref[...]
        wrh = wr.astype(BF16)
        wrh_s[...] = wrh
        wrl_s[...] = (wr - wrh.astype(F32)).astype(BF16)

    mix = _bdot(m_ref[...].astype(BF16), wo_s[...])
    x1 = _layer_norm(alpha * x0_ref[...] + mix, g_ref[...], b_ref[...])
    x1_ref[...] = x1
    xh = x1.astype(BF16)
    xhf = xh.astype(F32)
    bits = lax.bitcast_convert_type(xhf, jnp.uint32)
    xp_ref[...] = jnp.bitwise_or(bits[:, D // 2:], lax.shift_right_logical(bits[:, :D // 2], jnp.uint32(16)))
    xl = (x1 - xhf).astype(BF16)
    logits = _bdot(xh, wrh_s[...]) + _bdot(xh, wrl_s[...]) + _bdot(xl, wrh_s[...]) + br_ref[...]
    _, vals, idxs = _top_select(logits, TOP_K)
    es = [jnp.exp(v - vals[0]) for v in vals]
    den = es[0] + es[1] + es[2] + es[3]
    lane = _lane_iota(logits.shape)
    route = jnp.zeros(logits.shape, F32)
    for k in range(TOP_K):
        route = jnp.where(lane == k, idxs[k], route)
        route = jnp.where(lane == TOP_K + k, es[k] / den, route)
    route_ref[...] = route


def mix_ln_router(x0, m, w_o, ln_g, ln_b, w_r, b_r, alpha, tm=512):
    n = x0.shape[0]
    wr = jnp.pad(w_r, ((0, 0), (0, LANES - N_EXPERTS)))
    br = jnp.pad(b_r, (0, LANES - N_EXPERTS), constant_values=NEG).reshape(1, LANES)
    row = lambda w: pl.BlockSpec((tm, w), lambda i: (i, 0))
    full = lambda a: pl.BlockSpec(a.shape, lambda i: (0,) * a.ndim)
    g2, b2 = ln_g.reshape(1, D), ln_b.reshape(1, D)
    return pl.pallas_call(
        functools.partial(_mix_ln_router_kernel, alpha),
        grid=(n // tm,),
        in_specs=[row(D), row(D), full(w_o), full(g2), full(b2), full(wr), full(br)],
        out_specs=[row(D), row(D // 2), row(LANES)],
        out_shape=[jax.ShapeDtypeStruct((n, D), F32), jax.ShapeDtypeStruct((n, D // 2), jnp.uint32),
                   jax.ShapeDtypeStruct((n, LANES), F32)],
        scratch_shapes=[pltpu.VMEM((D, D), BF16), pltpu.VMEM((D, LANES), BF16), pltpu.VMEM((D, LANES), BF16)],
        compiler_params=_cparams(1), name="mix_ln_router",
    )(x0, m, w_o, g2, b2, wr, br)


def _moe_kernel(start_ref, count_ref, tok_ref, gate_ref, xp_ref, wgu_ref, bgu_ref, wdn_ref, bdn_ref,
                out_ref, xg_s, y_s):
    c, e = pl.program_id(0), pl.program_id(1)

    @pl.when(jnp.logical_and(c == 0, e == 0))
    def _():
        xg_s[...] = jnp.zeros(xg_s.shape, xg_s.dtype)

    @pl.when(e == 0)
    def _():
        out_ref[...] = jnp.zeros(out_ref.shape, out_ref.dtype)

    start = start_ref[c, e]
    count = count_ref[c, e]

    def block(blk, carry):
        base = start + blk * MOE_ROWS
        rows = jnp.minimum(MOE_ROWS, count - blk * MOE_ROWS)

        groups = (rows + 7) // 8

        def gather(i8, cc):
            i0 = pl.multiple_of(i8 * 8, 8)
            picked = [xp_ref[pl.ds(tok_ref[0, base + i0 + j], 1), :] for j in range(8)]
            xg_s[pl.ds(i0, 8), :] = jnp.concatenate(picked, axis=0)
            return cc

        lax.fori_loop(0, groups, gather, 0)
        u = xg_s[...]
        lo = lax.bitcast_convert_type(lax.shift_left(u, jnp.uint32(16)), F32)
        hi = lax.bitcast_convert_type(jnp.bitwise_and(u, jnp.uint32(0xFFFF0000)), F32)
        xb = jnp.concatenate([lo, hi], axis=1).astype(BF16)
        h = _bdot(xb, wgu_ref[...]) + bgu_ref[...]
        gate = jnp.minimum(h[:, :D], SWIGLU_LIMIT)
        lin = jnp.clip(h[:, D:], -SWIGLU_LIMIT, SWIGLU_LIMIT)
        act = gate * _sigmoid(SWIGLU_ALPHA * gate) * (lin + 1.0)
        y_s[...] = _bdot(act.astype(BF16), wdn_ref[...]) + bdn_ref[...]

        def scatter(i8, cc):
            for j in range(8):
                i = i8 * 8 + j
                t = tok_ref[0, base + i]
                g = jnp.where(i < rows, gate_ref[0, base + i], 0.0)
                out_ref[pl.ds(t, 1), :] = out_ref[pl.ds(t, 1), :] + g * y_s[pl.ds(i, 1), :]
            return cc

        lax.fori_loop(0, groups, scatter, 0)
        return carry

    lax.fori_loop(0, (count + MOE_ROWS - 1) // MOE_ROWS, block, 0)


def moe_ffn(xp, route, layer, w_gu16, b_gu, w_dn16, b_dn, n_chunks):
    n = xp.shape[0]
    ch = n // n_chunks
    n_as = ch * TOP_K
    e_ids = route[:, :TOP_K].astype(jnp.int32).reshape(n_chunks, n_as)
    gates = route[:, TOP_K:2 * TOP_K].reshape(n_chunks, n_as)
    tok = jnp.tile(jnp.arange(ch, dtype=jnp.int32)[:, None], (1, TOP_K)).reshape(1, n_as)
    order = jnp.argsort(e_ids, axis=1, stable=True)
    tok_s = jnp.take_along_axis(jnp.broadcast_to(tok, e_ids.shape), order, axis=1)
    gate_s = jnp.take_along_axis(gates, order, axis=1)
    counts = jnp.sum(e_ids[:, :, None] == jnp.arange(N_EXPERTS, dtype=jnp.int32)[None, None, :], axis=1).astype(jnp.int32)
    starts = (jnp.cumsum(counts, axis=1) - counts).astype(jnp.int32)
    n_pad = n_as + 8
    tok_s = jnp.pad(tok_s, ((0, 0), (0, 8))).reshape(n_chunks, 1, n_pad)
    gate_s = jnp.pad(gate_s, ((0, 0), (0, 8))).reshape(n_chunks, 1, n_pad)
    depth = w_gu16.shape[0]
    grid_spec = pltpu.PrefetchScalarGridSpec(
        num_scalar_prefetch=2,
        grid=(n_chunks, N_EXPERTS),
        in_specs=[
            pl.BlockSpec((None, 1, n_pad), lambda c, e, s, k: (c, 0, 0), memory_space=pltpu.SMEM),
            pl.BlockSpec((None, 1, n_pad), lambda c, e, s, k: (c, 0, 0), memory_space=pltpu.SMEM),
            pl.BlockSpec((ch, D // 2), lambda c, e, s, k: (c, 0)),
            pl.BlockSpec((None, None, D, 2 * D), lambda c, e, s, k: (layer, e, 0, 0)),
            pl.BlockSpec((None, None, 1, 2 * D), lambda c, e, s, k: (layer, e, 0, 0)),
            pl.BlockSpec((None, None, D, D), lambda c, e, s, k: (layer, e, 0, 0)),
            pl.BlockSpec((None, None, 1, D), lambda c, e, s, k: (layer, e, 0, 0)),
        ],
        out_specs=pl.BlockSpec((ch, D), lambda c, e, s, k: (c, 0)),
        scratch_shapes=[pltpu.VMEM((MOE_ROWS, D // 2), jnp.uint32), pltpu.VMEM((MOE_ROWS, D), F32)],
    )
    return pl.pallas_call(
        _moe_kernel, grid_spec=grid_spec,
        out_shape=jax.ShapeDtypeStruct((n, D), F32),
        compiler_params=_cparams(2), name="moe",
    )(starts, counts, tok_s, gate_s, xp, w_gu16, b_gu.reshape(depth, N_EXPERTS, 1, 2 * D), w_dn16,
      b_dn.reshape(depth, N_EXPERTS, 1, D))


def _ln_ple_kernel(alpha, x1_ref, moe_ref, p_ref, g_ref, b_ref, wg_ref, wp_ref, o_ref, wg_s, wp_s):
    @pl.when(pl.program_id(0) == 0)
    def _():
        wg_s[...] = wg_ref[...].astype(BF16)
        wp_s[...] = wp_ref[...].astype(BF16)

    x2 = _layer_norm(alpha * x1_ref[...] + moe_ref[...], g_ref[...], b_ref[...])
    gate = _sigmoid(_bdot(x2.astype(BF16), wg_s[...]))
    o_ref[...] = x2 + gate * _bdot(p_ref[...].astype(BF16), wp_s[...])


def ln_ple(x1, moe, p, ln_g, ln_b, w_gate, w_proj, alpha, tm=512):
    n = x1.shape[0]
    row = lambda w: pl.BlockSpec((tm, w), lambda i: (i, 0))
    full = lambda a: pl.BlockSpec(a.shape, lambda i: (0,) * a.ndim)
    g2, b2 = ln_g.reshape(1, D), ln_b.reshape(1, D)
    return pl.pallas_call(
        functools.partial(_ln_ple_kernel, alpha),
        grid=(n // tm,),
        in_specs=[row(D), row(D), row(PLE_DIM), full(g2), full(b2), full(w_gate), full(w_proj)],
        out_specs=row(D),
        out_shape=jax.ShapeDtypeStruct((n, D), F32),
        scratch_shapes=[pltpu.VMEM((D, D), BF16), pltpu.VMEM((PLE_DIM, D), BF16)],
        compiler_params=_cparams(1), name="ln_ple",
    )(x1, moe, p, g2, b2, w_gate, w_proj)


def _rg_core(seg, x, win_s, cw_ref, cb_ref, wa_s, ba_ref, wi_s, bi_ref, lam_ref, shifted_u, h_prev):
    tm = x.shape[0]
    proj = _bdot(x.astype(BF16), win_s[...])
    y = _gelu(proj[:, :D])
    u = proj[:, D:]
    cw = cw_ref[...]
    conv = cb_ref[...] + cw[3:4] * u
    for k in (1, 2, 3):
        conv = conv + cw[3 - k:4 - k] * shifted_u(u, k)
    nb = wa_s.shape[0]
    bd = D // nb
    ra, ri = [], []
    for n in range(nb):
        cbn = conv[:, n * bd:(n + 1) * bd].astype(BF16)
        ra.append(_bdot(cbn, wa_s[n]))
        ri.append(_bdot(cbn, wi_s[n]))
    r = _sigmoid(jnp.concatenate(ra, axis=1) + ba_ref[...])
    ig = _sigmoid(jnp.concatenate(ri, axis=1) + bi_ref[...])
    z = -lam_ref[...]
    softplus = jnp.maximum(z, 0.0) + jnp.log1p(jnp.exp(-jnp.abs(z)))
    log_a = (-RG_C) * r * softplus
    a = jnp.exp(log_a)
    b = jnp.sqrt(-jnp.tanh(log_a) * (a * a + 1.0)) * ig * conv
    row = _row_iota((tm, D))
    pos = row if seg is None else jnp.bitwise_and(row, seg - 1)
    s = 1
    while s < (tm if seg is None else seg):
        keep = pos >= s
        a_sh = jnp.where(keep, pltpu.roll(a, s, 0), 1.0)
        b_sh = jnp.where(keep, pltpu.roll(b, s, 0), 0.0)
        b = a * b_sh + b
        a = a * a_sh
        s *= 2
    h = b + a * h_prev
    return u, h, (h * y).astype(BF16)


def _rg_cast_weights(win_ref, wa_ref, wi_ref, win_s, wa_s, wi_s):
    win_s[...] = win_ref[...].astype(BF16)
    wa_s[...] = wa_ref[...].astype(BF16)
    wi_s[...] = wi_ref[...].astype(BF16)


def _rg_prompt_kernel(x_ref, win_ref, cw_ref, cb_ref, wa_ref, ba_ref, wi_ref, bi_ref, lam_ref,
                      hy_ref, ul_ref, hl_ref, win_s, wa_s, wi_s, ucar_s, hcar_s):
    b, ti = pl.program_id(0), pl.program_id(1)

    @pl.when(jnp.logical_and(b == 0, ti == 0))
    def _():
        _rg_cast_weights(win_ref, wa_ref, wi_ref, win_s, wa_s, wi_s)

    @pl.when(ti == 0)
    def _():
        ucar_s[...] = jnp.zeros(ucar_s.shape, F32)
        hcar_s[...] = jnp.zeros(hcar_s.shape, F32)

    ucar = ucar_s[...]
    row8 = _row_iota((8, D))

    def shifted_u(u, k):
        rolled = pltpu.roll(u, k, 0)
        head = jnp.where(row8 < k, pltpu.roll(ucar, k, 0), rolled[:8])
        return jnp.concatenate([head, rolled[8:]], axis=0)

    u, h, hy = _rg_core(None, x_ref[...], win_s, cw_ref, cb_ref, wa_s, ba_ref, wi_s, bi_ref, lam_ref,
                        shifted_u, hcar_s[7:8, :])
    hy_ref[...] = hy
    tm = u.shape[0]
    ucar_s[...] = u[tm - 8:]
    hcar_s[...] = h[tm - 8:]
    ul_ref[...] = u[tm - 8:]
    hl_ref[...] = h[tm - 8:]


def _rg_sample_kernel(seg, x_ref, p1_ref, p2_ref, p3_ref, h0_ref, win_ref, cw_ref, cb_ref, wa_ref, ba_ref,
                      wi_ref, bi_ref, lam_ref, hy_ref, u_ref, h_ref, win_s, wa_s, wi_s):
    @pl.when(pl.program_id(0) == 0)
    def _():
        _rg_cast_weights(win_ref, wa_ref, wi_ref, win_s, wa_s, wi_s)

    prevs = (None, p1_ref, p2_ref, p3_ref)
    pos = jnp.bitwise_and(_row_iota(x_ref.shape), seg - 1)

    def shifted_u(u, k):
        return jnp.where(pos >= k, pltpu.roll(u, k, 0), prevs[k][...])

    u, h, hy = _rg_core(seg, x_ref[...], win_s, cw_ref, cb_ref, wa_s, ba_ref, wi_s, bi_ref, lam_ref,
                        shifted_u, h0_ref[...])
    hy_ref[...] = hy
    u_ref[...] = u
    h_ref[...] = h


def _rg_weight_args(w_in, conv_w, conv_b, w_a, b_a, w_i, b_i, lam):
    return (w_in, conv_w, conv_b.reshape(1, D), w_a, b_a.reshape(1, D), w_i, b_i.reshape(1, D), lam.reshape(1, D))


def _rg_scratch(w_a):
    return [pltpu.VMEM((D, 2 * D), BF16), pltpu.VMEM(w_a.shape, BF16), pltpu.VMEM(w_a.shape, BF16)]


def rg_prompt(x, batch, seq, weights, tm=256):
    wargs = _rg_weight_args(*weights)
    nt = seq // tm
    full = lambda a: pl.BlockSpec(a.shape, lambda b, t: (0,) * a.ndim)
    tail = pl.BlockSpec((None, 8, D), lambda b, t: (b, 0, 0))
    hy, ul, hl = pl.pallas_call(
        _rg_prompt_kernel, grid=(batch, nt),
        in_specs=[pl.BlockSpec((tm, D), lambda b, t: (b * nt + t, 0))] + [full(a) for a in wargs],
        out_specs=[pl.BlockSpec((tm, D), lambda b, t: (b * nt + t, 0)), tail, tail],
        out_shape=[jax.ShapeDtypeStruct((batch * seq, D), BF16), jax.ShapeDtypeStruct((batch, 8, D), F32),
                   jax.ShapeDtypeStruct((batch, 8, D), F32)],
        scratch_shapes=_rg_scratch(weights[3]) + [pltpu.VMEM((8, D), F32), pltpu.VMEM((8, D), F32)],
        compiler_params=_cparams(2), name="rg_prompt",
    )(x, *wargs)
    return hy, ul[:, 5:], hl[:, 7]


def rg_sample(x, n_seq, seg, conv_state, h_state, weights, tm=256):
    wargs = _rg_weight_args(*weights)
    n = n_seq * seg
    tm = min(tm, n)
    prevs = []
    for k in (1, 2, 3):
        pk = jnp.zeros((n_seq, seg, D), F32).at[:, :k].set(conv_state[:, 3 - k:])
        prevs.append(pk.reshape(n, D))
    h0 = jnp.broadcast_to(h_state[:, None, :], (n_seq, seg, D)).reshape(n, D)
    row = pl.BlockSpec((tm, D), lambda i: (i, 0))
    full = lambda a: pl.BlockSpec(a.shape, lambda i: (0,) * a.ndim)
    hy, u, h = pl.pallas_call(
        functools.partial(_rg_sample_kernel, seg), grid=(n // tm,),
        in_specs=[row] * 5 + [full(a) for a in wargs],
        out_specs=[row, row, row],
        out_shape=[jax.ShapeDtypeStruct((n, D), BF16), jax.ShapeDtypeStruct((n, D), F32),
                   jax.ShapeDtypeStruct((n, D), F32)],
        scratch_shapes=_rg_scratch(weights[3]),
        compiler_params=_cparams(1), name="rg_sample",
    )(x, *prevs, h0, *wargs)
    u = u.reshape(n_seq, seg, D)
    return hy, u[:, seg - 3:], h.reshape(n_seq, seg, D)[:, seg - 1]


def _rope_tables(pos):
    inv = ROPE_THETA ** (-jnp.arange(ROT_HALF, dtype=F32) / ROT_HALF)
    ang = pos.astype(F32)[:, None] * inv[None, :]
    cos, sin = jnp.cos(ang), jnp.sin(ang)
    n = pos.shape[0]
    rest = HEAD_DIM - 2 * ROT_HALF
    c = jnp.concatenate([cos, cos, jnp.ones((n, rest), F32)], axis=1)
    s1 = jnp.concatenate([-sin, jnp.zeros((n, HEAD_DIM - ROT_HALF), F32)], axis=1)
    s2 = jnp.concatenate([jnp.zeros((n, ROT_HALF), F32), sin, jnp.zeros((n, rest), F32)], axis=1)
    return tuple(jnp.tile(t, (1, LANES // HEAD_DIM)) for t in (c, s1, s2))


def _nsa_proj_kernel(x_ref, w_ref, ct_ref, s1_ref, s2_ref,
                     qx_ref, kv32_ref, win32_ref, kv16_ref, win16_ref, gate_ref, w_s):
    @pl.when(pl.program_id(0) == 0)
    def _():
        w_s[...] = w_ref[...].astype(BF16)

    proj = _bdot(x_ref[...].astype(BF16), w_s[...])
    ct, s1, s2 = ct_ref[...], s1_ref[...], s2_ref[...]

    def rope(blk):
        reps = blk.shape[1] // LANES
        w = blk.shape[1]
        return (blk * jnp.tile(ct, (1, reps)) + pltpu.roll(blk, w - ROT_HALF, 1) * jnp.tile(s1, (1, reps))
                + pltpu.roll(blk, ROT_HALF, 1) * jnp.tile(s2, (1, reps)))

    lane_grp = _lane_iota((x_ref.shape[0], GD)) // HEAD_DIM
    scale = HEAD_DIM ** -0.5 * math.log2(math.e)
    for g in range(KV_GROUPS):
        qg = rope(proj[:, g * GD:(g + 1) * GD]) * scale
        for r in range(GROUP_SIZE):
            moved = qg if r == g else pltpu.roll(qg, ((g - r) % KV_GROUPS) * HEAD_DIM, 1)
            h = g * GROUP_SIZE + r
            qx_ref[:, h * GD:(h + 1) * GD] = jnp.where(lane_grp == g, moved, 0.0).astype(BF16)
    blocks = []
    for i in range(6):
        blk = proj[:, D + i * GD:D + (i + 1) * GD]
        blocks.append(rope(blk) if i % 2 == 0 else blk)
    kv = jnp.concatenate(blocks[:4], axis=1)
    win = jnp.concatenate(blocks[4:], axis=1)
    kv32_ref[...] = kv
    win32_ref[...] = win
    kv16_ref[...] = kv.astype(BF16)
    win16_ref[...] = win.astype(BF16)
    gate_ref[...] = _sigmoid(proj[:, D + 6 * GD:])


def nsa_proj(x, pos, w_in, tm=256):
    n = x.shape[0]
    w = jnp.pad(w_in, ((0, 0), (0, NSA_PROJ_PAD - NSA_PROJ)))
    tabs = _rope_tables(pos)
    row = lambda wd: pl.BlockSpec((tm, wd), lambda i: (i, 0))
    return pl.pallas_call(
        _nsa_proj_kernel, grid=(n // tm,),
        in_specs=[row(D), pl.BlockSpec(w.shape, lambda i: (0, 0)), row(LANES), row(LANES), row(LANES)],
        out_specs=[row(N_HEADS * GD), row(D), row(2 * GD), row(D), row(2 * GD), row(LANES)],
        out_shape=[jax.ShapeDtypeStruct((n, N_HEADS * GD), BF16), jax.ShapeDtypeStruct((n, D), F32),
                   jax.ShapeDtypeStruct((n, 2 * GD), F32), jax.ShapeDtypeStruct((n, D), BF16),
                   jax.ShapeDtypeStruct((n, 2 * GD), BF16), jax.ShapeDtypeStruct((n, LANES), F32)],
        scratch_shapes=[pltpu.VMEM((D, NSA_PROJ_PAD), BF16)],
        compiler_params=_cparams(1), name="nsa_proj",
    )(x, w, *tabs)


def _cmp_accumulate(acc_ref, m, x16, w1_ref, l):
    half = (CMP_BLOCK // 2) * HEAD_DIM
    off = l * HEAD_DIM if isinstance(l, int) else pl.multiple_of(l * HEAD_DIM, HEAD_DIM)
    top = w1_ref[pl.ds(off, HEAD_DIM), :]
    bot = w1_ref[pl.ds(half + off, HEAD_DIM), :]
    w = jnp.concatenate([top, bot], axis=1).astype(BF16)
    wrep = jnp.concatenate([w] * KV_GROUPS, axis=0)
    lane_grp = _lane_iota(x16.shape) // HEAD_DIM
    for g in range(KV_GROUPS):
        xg = jnp.where(lane_grp == g, x16, jnp.zeros_like(x16))
        acc_ref[m, g] = acc_ref[m, g] + _bdot(xg, wrep)


def _cmp_finalize(acc_ref, m, pe_ref, w1_ref, w2p_ref):
    nch = acc_ref.shape[2]
    pe8 = jnp.broadcast_to(pe_ref[...], (8, pe_ref.shape[1])).astype(BF16)
    bias = _bdot(pe8, w1_ref[...].astype(BF16))[0:1]
    out = jnp.zeros((nch, GD), F32)
    for g in range(KV_GROUPS):
        a = acc_ref[m, g]
        hid = a[:, :CMP_HIDDEN] + pltpu.roll(a[:, CMP_HIDDEN:], nch - 1, 0) + bias
        out = out + _bdot(_gelu(hid).astype(BF16), w2p_ref[g].astype(BF16))
    return out


def _cmp_prompt_kernel(kv_ref, w1k_ref, w1v_ref, pek_ref, pev_ref, w2k_ref, w2v_ref, kc_ref, vc_ref, acc_s):
    l = pl.program_id(1)

    @pl.when(l == 0)
    def _():
        acc_s[...] = jnp.zeros(acc_s.shape, F32)

    blk = kv_ref[...]
    _cmp_accumulate(acc_s, 0, blk[:, :GD], w1k_ref, l)
    _cmp_accumulate(acc_s, 1, blk[:, GD:2 * GD], w1v_ref, l)

    @pl.when(l == CMP_STRIDE - 1)
    def _():
        kc_ref[...] = _cmp_finalize(acc_s, 0, pek_ref, w1k_ref, w2k_ref).astype(BF16)
        vc_ref[...] = _cmp_finalize(acc_s, 1, pev_ref, w1v_ref, w2v_ref).astype(BF16)


def _w2_padded(w2):
    out = jnp.zeros((KV_GROUPS, CMP_HIDDEN, GD), F32)
    for g in range(KV_GROUPS):
        out = out.at[g, :, g * HEAD_DIM:(g + 1) * HEAD_DIM].set(w2)
    return out


def cmp_prompt(kv16, batch, seq, cmp_w):
    pe_k, w1k, w2k, pe_v, w1v, w2v = cmp_w
    nch = seq // CMP_STRIDE
    kvr = kv16.reshape(batch, nch, CMP_STRIDE * D)
    full = lambda a: pl.BlockSpec(a.shape, lambda b, l: (0,) * a.ndim)
    args = (w1k, w1v, pe_k.reshape(1, -1), pe_v.reshape(1, -1), _w2_padded(w2k), _w2_padded(w2v))
    out = pl.BlockSpec((None, nch, GD), lambda b, l: (b, 0, 0))
    return pl.pallas_call(
        _cmp_prompt_kernel, grid=(batch, CMP_STRIDE),
        in_specs=[pl.BlockSpec((None, nch, D), lambda b, l: (b, 0, l))] + [full(a) for a in args],
        out_specs=[out, out],
        out_shape=[jax.ShapeDtypeStruct((batch, nch, GD), BF16)] * 2,
        scratch_shapes=[pltpu.VMEM((2, KV_GROUPS, nch, 2 * CMP_HIDDEN), F32)],
        compiler_params=_cparams(2), name="cmp_prompt",
    )(kvr, *args)


def _overlap_matrix(n_c_rows, n_sel):
    c0 = jnp.arange(n_c_rows)[:, None] * CMP_STRIDE
    j0 = jnp.arange(LANES)[None, :] * SEL_BLOCK
    ov = (c0 < j0 + SEL_BLOCK) & (c0 + CMP_BLOCK > j0) & (jnp.arange(LANES)[None, :] < n_sel)
    return ov.astype(BF16)


def _masked_softmax(s, ok):
    s = jnp.where(ok, s, NEG)
    m = jnp.max(s, axis=-1, keepdims=True)
    e = jnp.where(ok, jnp.exp2(s - m), 0.0)
    return e, jnp.sum(e, axis=-1, keepdims=True)


def _importance_select(p_grp, ov_ref, tq):
    ph = p_grp.astype(BF16)
    pl_ = (p_grp - ph.astype(F32)).astype(BF16)
    imp = _bdot(ph, ov_ref[...]) + _bdot(pl_, ov_ref[...])
    jb = _lane_iota(imp.shape)
    jq = tq // SEL_BLOCK
    forced = jnp.logical_or(jb == 0, jnp.logical_or(jb == jq, jb == jq - 1))
    imp = jnp.where(forced, BIG, imp)
    imp = jnp.where(jb > jq, -BIG, imp)
    sel, _, _ = _top_select(imp, SEL_TOP)
    return sel


def _attn_prompt_kernel(tk, qx_ref, kc_ref, vc_ref, ks_ref, vs_ref, kw_ref, vw_ref, blk_ref, gate_ref, ov_ref,
                        o_ref, m_s, acc_s):
    qi, g = pl.program_id(1), pl.program_id(2)
    tq_n = qx_ref.shape[0]
    t0 = qi * tq_n
    heads = range(GROUP_SIZE)
    q = [qx_ref[:, r * GD:(r + 1) * GD] for r in heads]
    tq = t0 + _row_iota((tq_n, 1))

    kc, vc = kc_ref[...], vc_ref[...]
    visible = _lane_iota((tq_n, kc.shape[0])) * CMP_STRIDE + (CMP_BLOCK - 1) <= tq
    s = [_bdot_nt(qr, kc) for qr in q]
    e_den = [_masked_softmax(sr, visible) for sr in s]
    p_c = [e / jnp.maximum(den, 1e-30) for e, den in e_den]
    o_c = [_bdot(pr.astype(BF16), vc) for pr in p_c]
    sel = _importance_select((p_c[0] + p_c[1]) + (p_c[2] + p_c[3]), ov_ref, tq)

    own = _lane_iota((tq_n, GD)) // HEAD_DIM == g
    src = _row_iota((LANES, GD))
    dst = _lane_iota((LANES, GD))
    place = jnp.where(dst == (src + (g + 1) * HEAD_DIM) % GD, 1.0, 0.0).astype(BF16)
    penalty = _bdot(jnp.where(sel, 0.0, NEG).astype(BF16), place).astype(BF16)
    q_aug = [jnp.where(own, qr, penalty) for qr in q]
    own_k = _lane_iota((tk, GD)) // HEAD_DIM == g
    m_s[...] = jnp.full(m_s.shape, NEG, F32)
    acc_s[...] = jnp.zeros(acc_s.shape, F32)

    ones_grp = (g + 1) % KV_GROUPS

    def with_ones(v):
        return jnp.where(_lane_iota(v.shape) // HEAD_DIM == ones_grp, jnp.ones_like(v), v)

    def normalised(acc):
        den = jnp.sum(jnp.where(_lane_iota(acc.shape) == ones_grp * HEAD_DIM, acc, 0.0), axis=-1, keepdims=True)
        return acc / den

    def flash_step(qs, k, v, bias, ms, accs):
        s = [_bdot_nt(qr, k) for qr in qs]
        if bias is not None:
            s = [sr + bias for sr in s]
        m_new = [jnp.maximum(mo, jnp.max(sr, axis=-1, keepdims=True)) for mo, sr in zip(ms, s)]
        p = [jnp.exp2(sr - mn).astype(BF16) for sr, mn in zip(s, m_new)]
        a = [jnp.exp2(mo - mn) for mo, mn in zip(ms, m_new)]
        pv = [_bdot(pr, v) for pr in p]
        return m_new, [ar * acc + pvr for ar, acc, pvr in zip(a, accs, pv)]

    rows_of = [slice(r * tq_n, (r + 1) * tq_n) for r in heads]

    def key_tile(kt, causal_bias):
        k0 = pl.multiple_of(kt * tk, tk)
        ind = pltpu.bitcast(pltpu.roll(pltpu.bitcast(blk_ref[pl.ds(k0, tk), :], jnp.uint32), g * HEAD_DIM, 1), BF16)
        k_aug = jnp.where(own_k, ks_ref[pl.ds(k0, tk), :], ind)
        m_new, acc_new = flash_step(q_aug, k_aug, with_ones(vs_ref[pl.ds(k0, tk), :]), causal_bias,
                                    [m_s[x, :] for x in rows_of], [acc_s[x, :] for x in rows_of])
        for r, x in enumerate(rows_of):
            acc_s[x, :] = acc_new[r]
            m_s[x, :] = m_new[r]

    n_past = (t0 + tq_n + tk - 1) // tk - 1

    def past_tile(kt, carry):
        key_tile(kt, None)
        return carry

    lax.fori_loop(0, n_past, past_tile, 0)
    key_tile(n_past, jnp.where(n_past * tk + _lane_iota((tq_n, tk)) <= tq, 0.0, NEG))
    o_s = [normalised(acc_s[x, :]) for x in rows_of]

    w0 = pl.multiple_of(jnp.maximum(t0 - WINDOW, 0), tq_n)
    ms = [jnp.full((tq_n, 1), NEG, F32) for _ in heads]
    accs = [jnp.zeros((tq_n, GD), F32) for _ in heads]
    for off in range(0, WINDOW + tq_n, tk):
        width = min(tk, WINDOW + tq_n - off)
        dist = tq - (w0 + off + _lane_iota((tq_n, width)))
        bias = jnp.where(jnp.logical_and(dist >= 0, dist <= WINDOW), 0.0, NEG)
        ms, accs = flash_step(q, kw_ref[pl.ds(w0 + off, width), :], with_ones(vw_ref[pl.ds(w0 + off, width), :]),
                              bias, ms, accs)
    o_w = [normalised(acc) for acc in accs]

    gates = gate_ref[...]
    glane = _lane_iota(gates.shape)

    def gate_col(r, branch):
        return jnp.sum(jnp.where(glane == (g * GROUP_SIZE + r) * 3 + branch, gates, 0.0), axis=-1, keepdims=True)

    o = [gate_col(r, 0) * o_c[r] + gate_col(r, 1) * o_s[r] + gate_col(r, 2) * o_w[r] for r in heads]
    ocat = jnp.concatenate(o, axis=1).astype(BF16)
    src = _row_iota((GROUP_SIZE * GD, GD))
    dst = _lane_iota((GROUP_SIZE * GD, GD))
    move = src == (dst // HEAD_DIM) * GD + g * HEAD_DIM + dst % HEAD_DIM
    o_ref[...] = _bdot(ocat, jnp.where(move, 1.0, 0.0).astype(BF16)).astype(o_ref.dtype)


def attn_prompt(qx, kc, vc, kv16, win16, gates, batch, seq, tq=128, tk=512):
    nq = seq // tq
    nch = seq // CMP_STRIDE
    n_sel = -(-seq // SEL_BLOCK)
    assert n_sel <= LANES and seq % tk == 0 and tk % tq == 0
    ov = _overlap_matrix(nch, n_sel)
    blk = (jnp.arange(seq)[:, None] // SEL_BLOCK == jnp.arange(GD)[None, :] - HEAD_DIM).astype(BF16)
    kv3 = kv16.reshape(batch, seq, D)
    win3 = win16.reshape(batch, seq, 2 * GD)
    slab = lambda col: pl.BlockSpec((None, seq, GD), lambda b, i, g: (b, 0, col))
    cmp_spec = pl.BlockSpec((None, nch, GD), lambda b, i, g: (b, 0, 0))
    full = lambda a: pl.BlockSpec(a.shape, lambda b, i, g: (0, 0))
    return pl.pallas_call(
        functools.partial(_attn_prompt_kernel, tk), grid=(batch, nq, KV_GROUPS),
        in_specs=[pl.BlockSpec((tq, GROUP_SIZE * GD), lambda b, i, g: (b * nq + i, g)),
                  cmp_spec, cmp_spec, slab(2), slab(3), slab(0), slab(1), full(blk),
                  pl.BlockSpec((tq, LANES), lambda b, i, g: (b * nq + i, 0)), full(ov)],
        out_specs=pl.BlockSpec((tq, GD), lambda b, i, g: (b * nq + i, g)),
        out_shape=jax.ShapeDtypeStruct((batch * seq, D), BF16),
        scratch_shapes=[pltpu.VMEM((GROUP_SIZE * tq, 1), F32), pltpu.VMEM((GROUP_SIZE * tq, GD), F32)],
        compiler_params=_cparams(3), name="attn_prompt",
    )(qx, kc, vc, kv3, kv3, win3, win3, blk, gates, ov)


def _nsa_sample_kernel(n_pages, past, pt_ref, *refs):
    page_refs = refs[:n_pages]
    (st_ref, qx_ref, kvn_ref, winn_ref, gate_ref, w1k_ref, w1v_ref, pek_ref, pev_ref, w2k_ref, w2v_ref,
     ov_ref, move_ref, o_ref, nwin_ref, acc_s, rows_s) = refs[n_pages:]
    t_new = qx_ref.shape[0]
    rows = N_HEADS * t_new
    wb = st_ref.shape[1]

    acc_s[...] = jnp.zeros(acc_s.shape, F32)
    for i, p in enumerate(page_refs):
        for m in range(2):
            t = jnp.transpose(p[m * GD:(m + 1) * GD, :])
            for half in range(GD // LANES):
                rows_s[m, half, i * PAGE:(i + 1) * PAGE, :] = t[:, half * LANES:(half + 1) * LANES]
    n_chunk = past // CMP_STRIDE
    for l in range(CMP_STRIDE):
        for m, w1_ref in ((0, w1k_ref), (1, w1v_ref)):
            x = jnp.concatenate([rows_s[m, half, pl.ds(l, n_chunk, stride=CMP_STRIDE), :]
                                 for half in range(GD // LANES)], axis=1)
            _cmp_accumulate(acc_s, m, x.astype(BF16), w1_ref, l)
    kc = _cmp_finalize(acc_s, 0, pek_ref, w1k_ref, w2k_ref).astype(BF16)
    vc = _cmp_finalize(acc_s, 1, pev_ref, w1v_ref, w2v_ref).astype(BF16)

    qb = qx_ref[...]
    q = jnp.concatenate([qb[:, h * GD:(h + 1) * GD] for h in range(N_HEADS)], axis=0).astype(BF16)
    qidx = jnp.bitwise_and(_row_iota((rows, 1)), t_new - 1)
    tq = past + qidx

    s = _bdot_nt(q, kc)
    cend = _lane_iota(s.shape) * CMP_STRIDE + (CMP_BLOCK - 1)
    e, den = _masked_softmax(s, cend <= tq)
    p_c = e / jnp.maximum(den, 1e-30)
    o_c = _bdot(p_c.astype(BF16), vc)

    grp = []
    for g in range(KV_GROUPS):
        acc = p_c[g * GROUP_SIZE * t_new:(g * GROUP_SIZE + 1) * t_new]
        for r in range(1, GROUP_SIZE):
            h = g * GROUP_SIZE + r
            acc = acc + p_c[h * t_new:(h + 1) * t_new]
        grp.append(acc)
    p_grp = jnp.concatenate(grp, axis=0)
    sel = _importance_select(p_grp, ov_ref, past + jnp.bitwise_and(_row_iota((KV_GROUPS * t_new, 1)), t_new - 1))
    sel = jnp.where(sel, 1.0, 0.0).astype(BF16)
    sel_h = jnp.concatenate([sel[(h // GROUP_SIZE) * t_new:(h // GROUP_SIZE + 1) * t_new] for h in range(N_HEADS)],
                            axis=0)

    def new_keys(x):
        return jnp.concatenate([x, jnp.zeros((LANES - t_new, x.shape[1]), x.dtype)], axis=0).astype(BF16)

    new_ok = jnp.logical_and(_lane_iota((rows, LANES)) < t_new, _lane_iota((rows, LANES)) <= qidx)

    def two_part_attention(s_old, ok_old, vt_old, s_new, v_new):
        s_old = jnp.where(ok_old, s_old, NEG)
        s_new = jnp.where(new_ok, s_new, NEG)
        m = jnp.maximum(jnp.max(s_old, axis=-1, keepdims=True), jnp.max(s_new, axis=-1, keepdims=True))
        e_old = jnp.where(ok_old, jnp.exp2(s_old - m), 0.0)
        e_new = jnp.where(new_ok, jnp.exp2(s_new - m), 0.0)
        den = jnp.sum(e_old, axis=-1, keepdims=True) + jnp.sum(e_new, axis=-1, keepdims=True)
        return (_bdot_nt(e_old.astype(BF16), vt_old) + _bdot(e_new.astype(BF16), v_new)) / den

    kt_all = jnp.concatenate([p[2 * GD:3 * GD, :] for p in page_refs], axis=1).astype(BF16)
    vt_all = jnp.concatenate([p[3 * GD:4 * GD, :] for p in page_refs], axis=1).astype(BF16)
    kvn = kvn_ref[...]
    expand = _row_iota((LANES, past)) == _lane_iota((LANES, past)) // SEL_BLOCK
    ok_p = _bdot(sel_h, jnp.where(expand, 1.0, 0.0).astype(BF16)) > 0.5
    o_s = two_part_attention(_bdot(q, kt_all), ok_p, vt_all,
                             _bdot_nt(q, new_keys(kvn[:, 2 * GD:3 * GD])), new_keys(kvn[:, 3 * GD:4 * GD]))

    winn = winn_ref[...]
    dist = tq - (past - wb + _lane_iota((rows, wb)))
    ok_w = jnp.logical_and(dist >= 0, dist <= WINDOW)
    o_w = two_part_attention(_bdot(q, st_ref[:GD, :].astype(BF16)), ok_w, st_ref[GD:, :].astype(BF16),
                             _bdot_nt(q, new_keys(winn[:, :GD])), new_keys(winn[:, GD:]))

    gates = gate_ref[...]

    def gate_col(branch):
        return jnp.concatenate([gates[:, h * 3 + branch:h * 3 + branch + 1] for h in range(N_HEADS)], axis=0)

    o = gate_col(0) * o_c + gate_col(1) * o_s + gate_col(2) * o_w
    outs = []
    for g in range(KV_GROUPS):
        ocat = jnp.concatenate([o[(g * GROUP_SIZE + r) * t_new:(g * GROUP_SIZE + r + 1) * t_new]
                                for r in range(GROUP_SIZE)], axis=1).astype(BF16)
        outs.append(_bdot(ocat, move_ref[g]))
    o_ref[...] = jnp.concatenate(outs, axis=1)

    shifted = pltpu.roll(st_ref[...], wb - t_new, 1)
    new_t = jnp.transpose(jnp.concatenate([jnp.zeros((LANES - t_new, 2 * GD), F32), winn], axis=0))
    last = jnp.where(_lane_iota((2 * GD, LANES)) < LANES - t_new, shifted[:, wb - LANES:], new_t)
    nwin_ref[:, :wb - LANES] = shifted[:, :wb - LANES]
    nwin_ref[:, wb - LANES:] = last


def nsa_sample(cache_t, li, page_table, state_win_t, qx_s, kvn, winn, gates_s, cmp_w, past):
    pe_k, w1k, w2k, pe_v, w1v, w2v = cmp_w
    n_seq, n_pages = page_table.shape
    t_new = qx_s.shape[0] // n_seq
    wb = state_win_t.shape[3]
    assert wb >= LANES
    nch = past // CMP_STRIDE
    n_sel = -(-(past + t_new) // SEL_BLOCK)
    ov = _overlap_matrix(nch, n_sel)
    src = jnp.arange(GROUP_SIZE * GD)[None, :, None]
    dst = jnp.arange(GD)[None, None, :]
    grp = jnp.arange(KV_GROUPS)[:, None, None]
    move = (src == (dst // HEAD_DIM) * GD + grp * HEAD_DIM + dst % HEAD_DIM).astype(BF16)
    consts = (w1k, w1v, pe_k.reshape(1, -1), pe_v.reshape(1, -1), _w2_padded(w2k), _w2_padded(w2v), ov, move)
    full = lambda a: pl.BlockSpec(a.shape, lambda b, pt: (0,) * a.ndim)
    seq_rows = lambda w: pl.BlockSpec((t_new, w), lambda b, pt: (b, 0))
    page_spec = lambda p: pl.BlockSpec((None, None, D, PAGE), lambda b, pt: (li, pt[b, p], 0, 0))
    grid_spec = pltpu.PrefetchScalarGridSpec(
        num_scalar_prefetch=1, grid=(n_seq,),
        in_specs=[page_spec(p) for p in range(n_pages)]
        + [pl.BlockSpec((None, None, 2 * GD, wb), lambda b, pt: (li, b, 0, 0)),
           seq_rows(N_HEADS * GD), seq_rows(D), seq_rows(2 * GD), seq_rows(LANES)]
        + [full(a) for a in consts],
        out_specs=[seq_rows(D), pl.BlockSpec((None, 2 * GD, wb), lambda b, pt: (b, 0, 0))],
        scratch_shapes=[pltpu.VMEM((2, KV_GROUPS, nch, 2 * CMP_HIDDEN), F32),
                        pltpu.VMEM((2, GD // LANES, past, LANES), F32)],
    )
    return pl.pallas_call(
        functools.partial(_nsa_sample_kernel, n_pages, past), grid_spec=grid_spec,
        out_shape=[jax.ShapeDtypeStruct((n_seq * t_new, D), F32), jax.ShapeDtypeStruct((n_seq, 2 * GD, wb), F32)],
        compiler_params=_cparams(1), name="nsa_sample",
    )(page_table, *([cache_t] * n_pages), state_win_t, qx_s, kvn, winn, gates_s, *consts)


def _moe_chunks(n):
    chunks = 1
    while n // chunks > 2304 and (n // chunks) % 16 == 0:
        chunks *= 2
    return chunks


def kernel(x_prompt, x_sample, cache_kv, state_win, state_conv, state_h, page_table, p_prompt, p_sample, ln_g, ln_b, rg_w_in, rg_conv_w, rg_conv_b, rg_w_a, rg_b_a, rg_w_i, rg_b_i, rg_lam, rg_w_o, nsa_w_in, nsa_pe_k, nsa_ck_w1, nsa_ck_w2, nsa_pe_v, nsa_cv_w1, nsa_cv_w2, nsa_w_o, moe_w_r, moe_b_r, moe_w_gu, moe_b_gu, moe_w_dn, moe_b_dn, ple_w_proj, ple_w_gate):
    batch, seq = x_prompt.shape[:2]
    n_seq, t_new = x_sample.shape[:2]
    depth = ln_g.shape[0]
    past = page_table.shape[1] * PAGE
    wb = state_win.shape[2]
    assert seq >= wb and wb == min(WINDOW, past)
    alpha = (2 * depth) ** 0.25
    n_p, n_s = batch * seq, n_seq * t_new
    n = n_p + n_s
    x = jnp.concatenate([x_prompt.reshape(n_p, D), x_sample.reshape(n_s, D)], axis=0)
    p_all = jnp.concatenate([p_prompt.reshape(depth, n_p, PLE_DIM), p_sample.reshape(depth, n_s, PLE_DIM)], axis=1)
    pos = jnp.concatenate([jnp.tile(jnp.arange(seq), batch), past + jnp.tile(jnp.arange(t_new), n_seq)])
    cache_t = jnp.transpose(cache_kv, (0, 1, 3, 4, 5, 2)).reshape(cache_kv.shape[0], cache_kv.shape[1], D, PAGE)
    win_t = jnp.transpose(state_win, (0, 1, 3, 4, 5, 2)).reshape(state_win.shape[0], n_seq, 2 * GD, wb)
    w_gu16 = moe_w_gu.astype(BF16)
    w_dn16 = moe_w_dn.astype(BF16)
    n_chunks = _moe_chunks(n)
    kv_p, kv_s, win_p, win_s, conv_p, conv_s, h_p, h_s = [], [], [], [], [], [], [], []
    for i in range(depth):
        li = i // 2
        if i % 2 == 0:
            weights = (rg_w_in[li], rg_conv_w[li], rg_conv_b[li], rg_w_a[li], rg_b_a[li], rg_w_i[li], rg_b_i[li],
                       rg_lam[li])
            hy_p, cp, hp = rg_prompt(x[:n_p], batch, seq, weights)
            hy_s, cs, hs = rg_sample(x[n_p:], n_seq, t_new, state_conv[li], state_h[li], weights)
            conv_p.append(cp)
            conv_s.append(cs)
            h_p.append(hp)
            h_s.append(hs)
            mixed = jnp.concatenate([hy_p, hy_s], axis=0)
            w_o = rg_w_o[li]
        else:
            qx, kv32, win32, kv16, win16, gates = nsa_proj(x, pos, nsa_w_in[li])
            cmp_w = (nsa_pe_k[li], nsa_ck_w1[li], nsa_ck_w2[li], nsa_pe_v[li], nsa_cv_w1[li], nsa_cv_w2[li])
            kc, vc = cmp_prompt(kv16[:n_p], batch, seq, cmp_w)
            o_p = attn_prompt(qx[:n_p], kc, vc, kv16[:n_p], win16[:n_p], gates[:n_p], batch, seq)
            o_s, nwin = nsa_sample(cache_t, li, page_table, win_t, qx[n_p:].astype(F32), kv32[n_p:], win32[n_p:],
                                   gates[n_p:], cmp_w, past)
            kv_p.append(kv32[:n_p].reshape(batch, seq, 4, KV_GROUPS, HEAD_DIM))
            kv_s.append(kv32[n_p:].reshape(n_seq, t_new, 4, KV_GROUPS, HEAD_DIM))
            win_p.append(win32[:n_p].reshape(batch, seq, 2, KV_GROUPS, HEAD_DIM)[:, seq - wb:])
            win_s.append(jnp.transpose(nwin.reshape(n_seq, 2, KV_GROUPS, HEAD_DIM, wb), (0, 4, 1, 2, 3)))
            mixed = jnp.concatenate([o_p, o_s.astype(BF16)], axis=0)
            w_o = nsa_w_o[li]
        x1, xp, route = mix_ln_router(x, mixed, w_o, ln_g[i, 0], ln_b[i, 0], moe_w_r[i], moe_b_r[i], alpha)
        moe = moe_ffn(xp, route, i, w_gu16, moe_b_gu, w_dn16, moe_b_dn, n_chunks)
        x = ln_ple(x1, moe, p_all[i], ln_g[i, 1], ln_b[i, 1], ple_w_gate[i], ple_w_proj[i], alpha)
    return (x[:n_p].reshape(batch, seq, D), x[n_p:].reshape(n_seq, t_new, D),
            jnp.stack(kv_p), jnp.stack(kv_s), jnp.stack(win_p), jnp.stack(win_s),
            jnp.stack(conv_p), jnp.stack(conv_s), jnp.stack(h_p), jnp.stack(h_s))
```

```python
import functools
import math

import jax
import jax.numpy as jnp
from jax import lax
from jax.experimental import pallas as pl
from jax.experimental.pallas import tpu as pltpu

F32 = jnp.float32
BF16 = jnp.bfloat16

D = 1024
N_HEADS = 16
HEAD_DIM = 64
KV_GROUPS = 4
GROUP_SIZE = 4
GD = KV_GROUPS * HEAD_DIM
ROT_HALF = 8
ROPE_THETA = 500000.0
CMP_BLOCK = 32
CMP_STRIDE = 16
CMP_HIDDEN = 128
SEL_BLOCK = 64
SEL_TOP = 16
WINDOW = 512
N_EXPERTS = 32
TOP_K = 4
SWIGLU_LIMIT = 7.0
SWIGLU_ALPHA = 1.702
PLE_DIM = 256
PAGE = 128
RG_C = 8.0
LN_EPS = 1e-5
NEG = -1e30
BIG = 1e30
NSA_PROJ = 2608
NSA_PROJ_PAD = 2688
LANES = 128
MOE_ROWS = 144
VMEM_LIMIT = 56 * 1024 * 1024


def _cparams(n_axes):
    return pltpu.CompilerParams(dimension_semantics=("arbitrary",) * n_axes,
                                vmem_limit_bytes=VMEM_LIMIT)


def _bdot(a, b):
    return jnp.dot(a, b, preferred_element_type=F32)


def _bdot_nt(a, b):
    return lax.dot_general(a, b, (((1,), (1,)), ((), ())), preferred_element_type=F32)


def _gelu(x):
    return 0.5 * x * (1.0 + jnp.tanh(math.sqrt(2.0 / math.pi) * (x + 0.044715 * (x * x * x))))


def _sigmoid(x):
    return 1.0 / (1.0 + jnp.exp(-x))


def _layer_norm(y, g, b):
    mu = jnp.mean(y, axis=-1, keepdims=True)
    yc = y - mu
    var = jnp.mean(yc * yc, axis=-1, keepdims=True)
    return yc * lax.rsqrt(var + LN_EPS) * g + b


def _lane_iota(shape):
    return lax.broadcasted_iota(jnp.int32, shape, len(shape) - 1)


def _row_iota(shape):
    return lax.broadcasted_iota(jnp.int32, shape, 0)


def _top_select(v, n_pick):
    lane = _lane_iota(v.shape).astype(F32)
    sel = jnp.zeros(v.shape, jnp.bool_)
    vals, idxs = [], []
    for _ in range(n_pick):
        m = jnp.max(v, axis=-1, keepdims=True)
        j = jnp.min(jnp.where(v == m, lane, float(v.shape[-1])), axis=-1, keepdims=True)
        pick = lane == j
        sel = jnp.logical_or(sel, pick)
        v = jnp.where(pick, -jnp.inf, v)
        vals.append(m)
        idxs.append(j)
    return sel, vals, idxs


def _mix_ln_router_kernel(alpha, x0_ref, m_ref, wo_ref, g_ref, b_ref, wr_ref, br_ref,
                          x1_ref, xp_ref, route_ref, wo_s, wrh_s, wrl_s):
    @pl.when(pl.program_id(0) == 0)
    def _():
        wo_s[...] = wo_ref[...].astype(BF16)
        wr = wr_ref[...]
        wrh = wr.astype(BF16)
        wrh_s[...] = wrh
        wrl_s[...] = (wr - wrh.astype(F32)).astype(BF16)

    mix = _bdot(m_ref[...].astype(BF16), wo_s[...])
    x1 = _layer_norm(alpha * x0_ref[...] + mix, g_ref[...], b_ref[...])
    x1_ref[...] = x1
    xh = x1.astype(BF16)
    xhf = xh.astype(F32)
    bits = lax.bitcast_convert_type(xhf, jnp.uint32)
    xp_ref[...] = jnp.bitwise_or(bits[:, D // 2:], lax.shift_right_logical(bits[:, :D // 2], jnp.uint32(16)))
    xl = (x1 - xhf).astype(BF16)
    logits = _bdot(xh, wrh_s[...]) + _bdot(xh, wrl_s[...]) + _bdot(xl, wrh_s[...]) + br_ref[...]
    _, vals, idxs = _top_select(logits, TOP_K)
    es = [jnp.exp(v - vals[0]) for v in vals]
    den = es[0] + es[1] + es[2] + es[3]
    lane = _lane_iota(logits.shape)
    route = jnp.zeros(logits.shape, F32)
    for k in range(TOP_K):
        route = jnp.where(lane == k, idxs[k], route)
        route = jnp.where(lane == TOP_K + k, es[k] / den, route)
    route_ref[...] = route


def mix_ln_router(x0, m, w_o, ln_g, ln_b, w_r, b_r, alpha, tm=512):
    n = x0.shape[0]
    wr = jnp.pad(w_r, ((0, 0), (0, LANES - N_EXPERTS)))
    br = jnp.pad(b_r, (0, LANES - N_EXPERTS), constant_values=NEG).reshape(1, LANES)
    row = lambda w: pl.BlockSpec((tm, w), lambda i: (i, 0))
    full = lambda a: pl.BlockSpec(a.shape, lambda i: (0,) * a.ndim)
    g2, b2 = ln_g.reshape(1, D), ln_b.reshape(1, D)
    return pl.pallas_call(
        functools.partial(_mix_ln_router_kernel, alpha),
        grid=(n // tm,),
        in_specs=[row(D), row(D), full(w_o), full(g2), full(b2), full(wr), full(br)],
        out_specs=[row(D), row(D // 2), row(LANES)],
        out_shape=[jax.ShapeDtypeStruct((n, D), F32), jax.ShapeDtypeStruct((n, D // 2), jnp.uint32),
                   jax.ShapeDtypeStruct((n, LANES), F32)],
        scratch_shapes=[pltpu.VMEM((D, D), BF16), pltpu.VMEM((D, LANES), BF16), pltpu.VMEM((D, LANES), BF16)],
        compiler_params=_cparams(1), name="mix_ln_router",
    )(x0, m, w_o, g2, b2, wr, br)


def _moe_kernel(start_ref, count_ref, tok_ref, gate_ref, xp_ref, wgu_ref, bgu_ref, wdn_ref, bdn_ref,
                out_ref, xg_s, y_s):
    c, e = pl.program_id(0), pl.program_id(1)

    @pl.when(jnp.logical_and(c == 0, e == 0))
    def _():
        xg_s[...] = jnp.zeros(xg_s.shape, xg_s.dtype)

    @pl.when(e == 0)
    def _():
        out_ref[...] = jnp.zeros(out_ref.shape, out_ref.dtype)

    start = start_ref[c, e]
    count = count_ref[c, e]

    def block(blk, carry):
        base = start + blk * MOE_ROWS
        rows = jnp.minimum(MOE_ROWS, count - blk * MOE_ROWS)

        groups = (rows + 7) // 8

        def gather(i8, cc):
            i0 = pl.multiple_of(i8 * 8, 8)
            picked = [xp_ref[pl.ds(tok_ref[0, base + i0 + j], 1), :] for j in range(8)]
            xg_s[pl.ds(i0, 8), :] = jnp.concatenate(picked, axis=0)
            return cc

        lax.fori_loop(0, groups, gather, 0)
        u = xg_s[...]
        lo = lax.bitcast_convert_type(lax.shift_left(u, jnp.uint32(16)), F32)
        hi = lax.bitcast_convert_type(jnp.bitwise_and(u, jnp.uint32(0xFFFF0000)), F32)
        xb = jnp.concatenate([lo, hi], axis=1).astype(BF16)
        h = _bdot(xb, wgu_ref[...]) + bgu_ref[...]
        gate = jnp.minimum(h[:, :D], SWIGLU_LIMIT)
        lin = jnp.clip(h[:, D:], -SWIGLU_LIMIT, SWIGLU_LIMIT)
        act = gate * _sigmoid(SWIGLU_ALPHA * gate) * (lin + 1.0)
        y_s[...] = _bdot(act.astype(BF16), wdn_ref[...]) + bdn_ref[...]

        def scatter(i8, cc):
            for j in range(8):
                i = i8 * 8 + j
                t = tok_ref[0, base + i]
                g = jnp.where(i < rows, gate_ref[0, base + i], 0.0)
                out_ref[pl.ds(t, 1), :] = out_ref[pl.ds(t, 1), :] + g * y_s[pl.ds(i, 1), :]
            return cc

        lax.fori_loop(0, groups, scatter, 0)
        return carry

    lax.fori_loop(0, (count + MOE_ROWS - 1) // MOE_ROWS, block, 0)


def moe_ffn(xp, route, layer, w_gu16, b_gu, w_dn16, b_dn, n_chunks):
    n = xp.shape[0]
    ch = n // n_chunks
    n_as = ch * TOP_K
    e_ids = route[:, :TOP_K].astype(jnp.int32).reshape(n_chunks, n_as)
    gates = route[:, TOP_K:2 * TOP_K].reshape(n_chunks, n_as)
    tok = jnp.tile(jnp.arange(ch, dtype=jnp.int32)[:, None], (1, TOP_K)).reshape(1, n_as)
    order = jnp.argsort(e_ids, axis=1, stable=True)
    tok_s = jnp.take_along_axis(jnp.broadcast_to(tok, e_ids.shape), order, axis=1)
    gate_s = jnp.take_along_axis(gates, order, axis=1)
    counts = jnp.sum(e_ids[:, :, None] == jnp.arange(N_EXPERTS, dtype=jnp.int32)[None, None, :], axis=1).astype(jnp.int32)
    starts = (jnp.cumsum(counts, axis=1) - counts).astype(jnp.int32)
    n_pad = n_as + 8
    tok_s = jnp.pad(tok_s, ((0, 0), (0, 8))).reshape(n_chunks, 1, n_pad)
    gate_s = jnp.pad(gate_s, ((0, 0), (0, 8))).reshape(n_chunks, 1, n_pad)
    depth = w_gu16.shape[0]
    grid_spec = pltpu.PrefetchScalarGridSpec(
        num_scalar_prefetch=2,
        grid=(n_chunks, N_EXPERTS),
        in_specs=[
            pl.BlockSpec((None, 1, n_pad), lambda c, e, s, k: (c, 0, 0), memory_space=pltpu.SMEM),
            pl.BlockSpec((None, 1, n_pad), lambda c, e, s, k: (c, 0, 0), memory_space=pltpu.SMEM),
            pl.BlockSpec((ch, D // 2), lambda c, e, s, k: (c, 0)),
            pl.BlockSpec((None, None, D, 2 * D), lambda c, e, s, k: (layer, e, 0, 0)),
            pl.BlockSpec((None, None, 1, 2 * D), lambda c, e, s, k: (layer, e, 0, 0)),
            pl.BlockSpec((None, None, D, D), lambda c, e, s, k: (layer, e, 0, 0)),
            pl.BlockSpec((None, None, 1, D), lambda c, e, s, k: (layer, e, 0, 0)),
        ],
        out_specs=pl.BlockSpec((ch, D), lambda c, e, s, k: (c, 0)),
        scratch_shapes=[pltpu.VMEM((MOE_ROWS, D // 2), jnp.uint32), pltpu.VMEM((MOE_ROWS, D), F32)],
    )
    return pl.pallas_call(
        _moe_kernel, grid_spec=grid_spec,
        out_shape=jax.ShapeDtypeStruct((n, D), F32),
        compiler_params=_cparams(2), name="moe",
    )(starts, counts, tok_s, gate_s, xp, w_gu16, b_gu.reshape(depth, N_EXPERTS, 1, 2 * D), w_dn16,
      b_dn.reshape(depth, N_EXPERTS, 1, D))


def _ln_ple_kernel(alpha, x1_ref, moe_ref, p_ref, g_ref, b_ref, wg_ref, wp_ref, o_ref, wg_s, wp_s):
    @pl.when(pl.program_id(0) == 0)
    def _():
        wg_s[...] = wg_ref[...].astype(BF16)
        wp_s[...] = wp_ref[...].astype(BF16)

    x2 = _layer_norm(alpha * x1_ref[...] + moe_ref[...], g_ref[...], b_ref[...])
    gate = _sigmoid(_bdot(x2.astype(BF16), wg_s[...]))
    o_ref[...] = x2 + gate * _bdot(p_ref[...].astype(BF16), wp_s[...])


def ln_ple(x1, moe, p, ln_g, ln_b, w_gate, w_proj, alpha, tm=512):
    n = x1.shape[0]
    row = lambda w: pl.BlockSpec((tm, w), lambda i: (i, 0))
    full = lambda a: pl.BlockSpec(a.shape, lambda i: (0,) * a.ndim)
    g2, b2 = ln_g.reshape(1, D), ln_b.reshape(1, D)
    return pl.pallas_call(
        functools.partial(_ln_ple_kernel, alpha),
        grid=(n // tm,),
        in_specs=[row(D), row(D), row(PLE_DIM), full(g2), full(b2), full(w_gate), full(w_proj)],
        out_specs=row(D),
        out_shape=jax.ShapeDtypeStruct((n, D), F32),
        scratch_shapes=[pltpu.VMEM((D, D), BF16), pltpu.VMEM((PLE_DIM, D), BF16)],
        compiler_params=_cparams(1), name="ln_ple",
    )(x1, moe, p, g2, b2, w_gate, w_proj)


def _rg_core(seg, x, win_s, cw_ref, cb_ref, wa_s, ba_ref, wi_s, bi_ref, lam_ref, shifted_u, h_prev):
    tm = x.shape[0]
    proj = _bdot(x.astype(BF16), win_s[...])
    y = _gelu(proj[:, :D])
    u = proj[:, D:]
    cw = cw_ref[...]
    conv = cb_ref[...] + cw[3:4] * u
    for k in (1, 2, 3):
        conv = conv + cw[3 - k:4 - k] * shifted_u(u, k)
    nb = wa_s.shape[0]
    bd = D // nb
    ra, ri = [], []
    for n in range(nb):
        cbn = conv[:, n * bd:(n + 1) * bd].astype(BF16)
        ra.append(_bdot(cbn, wa_s[n]))
        ri.append(_bdot(cbn, wi_s[n]))
    r = _sigmoid(jnp.concatenate(ra, axis=1) + ba_ref[...])
    ig = _sigmoid(jnp.concatenate(ri, axis=1) + bi_ref[...])
    z = -lam_ref[...]
    softplus = jnp.maximum(z, 0.0) + jnp.log1p(jnp.exp(-jnp.abs(z)))
    log_a = (-RG_C) * r * softplus
    a = jnp.exp(log_a)
    b = jnp.sqrt(-jnp.tanh(log_a) * (a * a + 1.0)) * ig * conv
    row = _row_iota((tm, D))
    pos = row if seg is None else jnp.bitwise_and(row, seg - 1)
    s = 1
    while s < (tm if seg is None else seg):
        keep = pos >= s
        a_sh = jnp.where(keep, pltpu.roll(a, s, 0), 1.0)
        b_sh = jnp.where(keep, pltpu.roll(b, s, 0), 0.0)
        b = a * b_sh + b
        a = a * a_sh
        s *= 2
    h = b + a * h_prev
    return u, h, (h * y).astype(BF16)


def _rg_cast_weights(win_ref, wa_ref, wi_ref, win_s, wa_s, wi_s):
    win_s[...] = win_ref[...].astype(BF16)
    wa_s[...] = wa_ref[...].astype(BF16)
    wi_s[...] = wi_ref[...].astype(BF16)


def _rg_prompt_kernel(x_ref, win_ref, cw_ref, cb_ref, wa_ref, ba_ref, wi_ref, bi_ref, lam_ref,
                      hy_ref, ul_ref, hl_ref, win_s, wa_s, wi_s, ucar_s, hcar_s):
    b, ti = pl.program_id(0), pl.program_id(1)

    @pl.when(jnp.logical_and(b == 0, ti == 0))
    def _():
        _rg_cast_weights(win_ref, wa_ref, wi_ref, win_s, wa_s, wi_s)

    @pl.when(ti == 0)
    def _():
        ucar_s[...] = jnp.zeros(ucar_s.shape, F32)
        hcar_s[...] = jnp.zeros(hcar_s.shape, F32)

    ucar = ucar_s[...]
    row8 = _row_iota((8, D))

    def shifted_u(u, k):
        rolled = pltpu.roll(u, k, 0)
        head = jnp.where(row8 < k, pltpu.roll(ucar, k, 0), rolled[:8])
        return jnp.concatenate([head, rolled[8:]], axis=0)

    u, h, hy = _rg_core(None, x_ref[...], win_s, cw_ref, cb_ref, wa_s, ba_ref, wi_s, bi_ref, lam_ref,
                        shifted_u, hcar_s[7:8, :])
    hy_ref[...] = hy
    tm = u.shape[0]
    ucar_s[...] = u[tm - 8:]
    hcar_s[...] = h[tm - 8:]
    ul_ref[...] = u[tm - 8:]
    hl_ref[...] = h[tm - 8:]


def _rg_sample_kernel(seg, x_ref, p1_ref, p2_ref, p3_ref, h0_ref, win_ref, cw_ref, cb_ref, wa_ref, ba_ref,
                      wi_ref, bi_ref, lam_ref, hy_ref, u_ref, h_ref, win_s, wa_s, wi_s):
    @pl.when(pl.program_id(0) == 0)
    def _():
        _rg_cast_weights(win_ref, wa_ref, wi_ref, win_s, wa_s, wi_s)

    prevs = (None, p1_ref, p2_ref, p3_ref)
    pos = jnp.bitwise_and(_row_iota(x_ref.shape), seg - 1)

    def shifted_u(u, k):
        return jnp.where(pos >= k, pltpu.roll(u, k, 0), prevs[k][...])

    u, h, hy = _rg_core(seg, x_ref[...], win_s, cw_ref, cb_ref, wa_s, ba_ref, wi_s, bi_ref, lam_ref,
                        shifted_u, h0_ref[...])
    hy_ref[...] = hy
    u_ref[...] = u
    h_ref[...] = h


def _rg_weight_args(w_in, conv_w, conv_b, w_a, b_a, w_i, b_i, lam):
    return (w_in, conv_w, conv_b.reshape(1, D), w_a, b_a.reshape(1, D), w_i, b_i.reshape(1, D), lam.reshape(1, D))


def _rg_scratch(w_a):
    return [pltpu.VMEM((D, 2 * D), BF16), pltpu.VMEM(w_a.shape, BF16), pltpu.VMEM(w_a.shape, BF16)]


def rg_prompt(x, batch, seq, weights, tm=256):
    wargs = _rg_weight_args(*weights)
    nt = seq // tm
    full = lambda a: pl.BlockSpec(a.shape, lambda b, t: (0,) * a.ndim)
    tail = pl.BlockSpec((None, 8, D), lambda b, t: (b, 0, 0))
    hy, ul, hl = pl.pallas_call(
        _rg_prompt_kernel, grid=(batch, nt),
        in_specs=[pl.BlockSpec((tm, D), lambda b, t: (b * nt + t, 0))] + [full(a) for a in wargs],
        out_specs=[pl.BlockSpec((tm, D), lambda b, t: (b * nt + t, 0)), tail, tail],
        out_shape=[jax.ShapeDtypeStruct((batch * seq, D), BF16), jax.ShapeDtypeStruct((batch, 8, D), F32),
                   jax.ShapeDtypeStruct((batch, 8, D), F32)],
        scratch_shapes=_rg_scratch(weights[3]) + [pltpu.VMEM((8, D), F32), pltpu.VMEM((8, D), F32)],
        compiler_params=_cparams(2), name="rg_prompt",
    )(x, *wargs)
    return hy, ul[:, 5:], hl[:, 7]


def rg_sample(x, n_seq, seg, conv_state, h_state, weights, tm=256):
    wargs = _rg_weight_args(*weights)
    n = n_seq * seg
    tm = min(tm, n)
    prevs = []
    for k in (1, 2, 3):
        pk = jnp.zeros((n_seq, seg, D), F32).at[:, :k].set(conv_state[:, 3 - k:])
        prevs.append(pk.reshape(n, D))
    h0 = jnp.broadcast_to(h_state[:, None, :], (n_seq, seg, D)).reshape(n, D)
    row = pl.BlockSpec((tm, D), lambda i: (i, 0))
    full = lambda a: pl.BlockSpec(a.shape, lambda i: (0,) * a.ndim)
    hy, u, h = pl.pallas_call(
        functools.partial(_rg_sample_kernel, seg), grid=(n // tm,),
        in_specs=[row] * 5 + [full(a) for a in wargs],
        out_specs=[row, row, row],
        out_shape=[jax.ShapeDtypeStruct((n, D), BF16), jax.ShapeDtypeStruct((n, D), F32),
                   jax.ShapeDtypeStruct((n, D), F32)],
        scratch_shapes=_rg_scratch(weights[3]),
        compiler_params=_cparams(1), name="rg_sample",
    )(x, *prevs, h0, *wargs)
    u = u.reshape(n_seq, seg, D)
    return hy, u[:, seg - 3:], h.reshape(n_seq, seg, D)[:, seg - 1]


def _rope_tables(pos):
    inv = ROPE_THETA ** (-jnp.arange(ROT_HALF, dtype=F32) / ROT_HALF)
    ang = pos.astype(F32)[:, None] * inv[None, :]
    cos, sin = jnp.cos(ang), jnp.sin(ang)
    n = pos.shape[0]
    rest = HEAD_DIM - 2 * ROT_HALF
    c = jnp.concatenate([cos, cos, jnp.ones((n, rest), F32)], axis=1)
    s1 = jnp.concatenate([-sin, jnp.zeros((n, HEAD_DIM - ROT_HALF), F32)], axis=1)
    s2 = jnp.concatenate([jnp.zeros((n, ROT_HALF), F32), sin, jnp.zeros((n, rest), F32)], axis=1)
    return tuple(jnp.tile(t, (1, LANES // HEAD_DIM)) for t in (c, s1, s2))


def _nsa_proj_kernel(x_ref, w_ref, ct_ref, s1_ref, s2_ref,
                     qx_ref, kv32_ref, win32_ref, kv16_ref, win16_ref, gate_ref, w_s):
    @pl.when(pl.program_id(0) == 0)
    def _():
        w_s[...] = w_ref[...].astype(BF16)

    proj = _bdot(x_ref[...].astype(BF16), w_s[...])
    ct, s1, s2 = ct_ref[...], s1_ref[...], s2_ref[...]

    def rope(blk):
        reps = blk.shape[1] // LANES
        w = blk.shape[1]
        return (blk * jnp.tile(ct, (1, reps)) + pltpu.roll(blk, w - ROT_HALF, 1) * jnp.tile(s1, (1, reps))
                + pltpu.roll(blk, ROT_HALF, 1) * jnp.tile(s2, (1, reps)))

    lane_grp = _lane_iota((x_ref.shape[0], GD)) // HEAD_DIM
    scale = HEAD_DIM ** -0.5 * math.log2(math.e)
    for g in range(KV_GROUPS):
        qg = rope(proj[:, g * GD:(g + 1) * GD]) * scale
        for r in range(GROUP_SIZE):
            moved = qg if r == g else pltpu.roll(qg, ((g - r) % KV_GROUPS) * HEAD_DIM, 1)
            h = g * GROUP_SIZE + r
            qx_ref[:, h * GD:(h + 1) * GD] = jnp.where(lane_grp == g, moved, 0.0).astype(BF16)
    blocks = []
    for i in range(6):
        blk = proj[:, D + i * GD:D + (i + 1) * GD]
        blocks.append(rope(blk) if i % 2 == 0 else blk)
    kv = jnp.concatenate(blocks[:4], axis=1)
    win = jnp.concatenate(blocks[4:], axis=1)
    kv32_ref[...] = kv
    win32_ref[...] = win
    kv16_ref[...] = kv.astype(BF16)
    win16_ref[...] = win.astype(BF16)
    gate_ref[...] = _sigmoid(proj[:, D + 6 * GD:])


def nsa_proj(x, pos, w_in, tm=256):
    n = x.shape[0]
    w = jnp.pad(w_in, ((0, 0), (0, NSA_PROJ_PAD - NSA_PROJ)))
    tabs = _rope_tables(pos)
    row = lambda wd: pl.BlockSpec((tm, wd), lambda i: (i, 0))
    return pl.pallas_call(
        _nsa_proj_kernel, grid=(n // tm,),
        in_specs=[row(D), pl.BlockSpec(w.shape, lambda i: (0, 0)), row(LANES), row(LANES), row(LANES)],
        out_specs=[row(N_HEADS * GD), row(D), row(2 * GD), row(D), row(2 * GD), row(LANES)],
        out_shape=[jax.ShapeDtypeStruct((n, N_HEADS * GD), BF16), jax.ShapeDtypeStruct((n, D), F32),
                   jax.ShapeDtypeStruct((n, 2 * GD), F32), jax.ShapeDtypeStruct((n, D), BF16),
                   jax.ShapeDtypeStruct((n, 2 * GD), BF16), jax.ShapeDtypeStruct((n, LANES), F32)],
        scratch_shapes=[pltpu.VMEM((D, NSA_PROJ_PAD), BF16)],
        compiler_params=_cparams(1), name="nsa_proj",
    )(x, w, *tabs)


def _cmp_accumulate(acc_ref, m, x16, w1_ref, l):
    half = (CMP_BLOCK // 2) * HEAD_DIM
    off = l * HEAD_DIM if isinstance(l, int) else pl.multiple_of(l * HEAD_DIM, HEAD_DIM)
    top = w1_ref[pl.ds(off, HEAD_DIM), :]
    bot = w1_ref[pl.ds(half + off, HEAD_DIM), :]
    w = jnp.concatenate([top, bot], axis=1).astype(BF16)
    wrep = jnp.concatenate([w] * KV_GROUPS, axis=0)
    lane_grp = _lane_iota(x16.shape) // HEAD_DIM
    for g in range(KV_GROUPS):
        xg = jnp.where(lane_grp == g, x16, jnp.zeros_like(x16))
        acc_ref[m, g] = acc_ref[m, g] + _bdot(xg, wrep)


def _cmp_finalize(acc_ref, m, pe_ref, w1_ref, w2p_ref):
    nch = acc_ref.shape[2]
    pe8 = jnp.broadcast_to(pe_ref[...], (8, pe_ref.shape[1])).astype(BF16)
    bias = _bdot(pe8, w1_ref[...].astype(BF16))[0:1]
    out = jnp.zeros((nch, GD), F32)
    for g in range(KV_GROUPS):
        a = acc_ref[m, g]
        hid = a[:, :CMP_HIDDEN] + pltpu.roll(a[:, CMP_HIDDEN:], nch - 1, 0) + bias
        out = out + _bdot(_gelu(hid).astype(BF16), w2p_ref[g].astype(BF16))
    return out


def _cmp_prompt_kernel(kv_ref, w1k_ref, w1v_ref, pek_ref, pev_ref, w2k_ref, w2v_ref, kc_ref, vc_ref, acc_s):
    l = pl.program_id(1)

    @pl.when(l == 0)
    def _():
        acc_s[...] = jnp.zeros(acc_s.shape, F32)

    blk = kv_ref[...]
    _cmp_accumulate(acc_s, 0, blk[:, :GD], w1k_ref, l)
    _cmp_accumulate(acc_s, 1, blk[:, GD:2 * GD], w1v_ref, l)

    @pl.when(l == CMP_STRIDE - 1)
    def _():
        kc_ref[...] = _cmp_finalize(acc_s, 0, pek_ref, w1k_ref, w2k_ref).astype(BF16)
        vc_ref[...] = _cmp_finalize(acc_s, 1, pev_ref, w1v_ref, w2v_ref).astype(BF16)


def _w2_padded(w2):
    out = jnp.zeros((KV_GROUPS, CMP_HIDDEN, GD), F32)
    for g in range(KV_GROUPS):
        out = out.at[g, :, g * HEAD_DIM:(g + 1) * HEAD_DIM].set(w2)
    return out


def cmp_prompt(kv16, batch, seq, cmp_w):
    pe_k, w1k, w2k, pe_v, w1v, w2v = cmp_w
    nch = seq // CMP_STRIDE
    kvr = kv16.reshape(batch, nch, CMP_STRIDE * D)
    full = lambda a: pl.BlockSpec(a.shape, lambda b, l: (0,) * a.ndim)
    args = (w1k, w1v, pe_k.reshape(1, -1), pe_v.reshape(1, -1), _w2_padded(w2k), _w2_padded(w2v))
    out = pl.BlockSpec((None, nch, GD), lambda b, l: (b, 0, 0))
    return pl.pallas_call(
        _cmp_prompt_kernel, grid=(batch, CMP_STRIDE),
        in_specs=[pl.BlockSpec((None, nch, D), lambda b, l: (b, 0, l))] + [full(a) for a in args],
        out_specs=[out, out],
        out_shape=[jax.ShapeDtypeStruct((batch, nch, GD), BF16)] * 2,
        scratch_shapes=[pltpu.VMEM((2, KV_GROUPS, nch, 2 * CMP_HIDDEN), F32)],
        compiler_params=_cparams(2), name="cmp_prompt",
    )(kvr, *args)


def _overlap_matrix(n_c_rows, n_sel):
    c0 = jnp.arange(n_c_rows)[:, None] * CMP_STRIDE
    j0 = jnp.arange(LANES)[None, :] * SEL_BLOCK
    ov = (c0 < j0 + SEL_BLOCK) & (c0 + CMP_BLOCK > j0) & (jnp.arange(LANES)[None, :] < n_sel)
    return ov.astype(BF16)


def _masked_softmax(s, ok):
    s = jnp.where(ok, s, NEG)
    m = jnp.max(s, axis=-1, keepdims=True)
    e = jnp.where(ok, jnp.exp2(s - m), 0.0)
    return e, jnp.sum(e, axis=-1, keepdims=True)


def _importance_select(p_grp, ov_ref, tq):
    ph = p_grp.astype(BF16)
    pl_ = (p_grp - ph.astype(F32)).astype(BF16)
    imp = _bdot(ph, ov_ref[...]) + _bdot(pl_, ov_ref[...])
    jb = _lane_iota(imp.shape)
    jq = tq // SEL_BLOCK
    forced = jnp.logical_or(jb == 0, jnp.logical_or(jb == jq, jb == jq - 1))
    imp = jnp.where(forced, BIG, imp)
    imp = jnp.where(jb > jq, -BIG, imp)
    sel, _, _ = _top_select(imp, SEL_TOP)
    return sel


def _attn_prompt_kernel(tk, qx_ref, kc_ref, vc_ref, ks_ref, vs_ref, kw_ref, vw_ref, blk_ref, gate_ref, ov_ref,
                        o_ref, m_s, acc_s):
    qi, g = pl.program_id(1), pl.program_id(2)
    tq_n = qx_ref.shape[0]
    t0 = qi * tq_n
    heads = range(GROUP_SIZE)
    q = [qx_ref[:, r * GD:(r + 1) * GD] for r in heads]
    tq = t0 + _row_iota((tq_n, 1))

    kc, vc = kc_ref[...], vc_ref[...]
    visible = _lane_iota((tq_n, kc.shape[0])) * CMP_STRIDE + (CMP_BLOCK - 1) <= tq
    s = [_bdot_nt(qr, kc) for qr in q]
    e_den = [_masked_softmax(sr, visible) for sr in s]
    p_c = [e / jnp.maximum(den, 1e-30) for e, den in e_den]
    o_c = [_bdot(pr.astype(BF16), vc) for pr in p_c]
    sel = _importance_select((p_c[0] + p_c[1]) + (p_c[2] + p_c[3]), ov_ref, tq)

    own = _lane_iota((tq_n, GD)) // HEAD_DIM == g
    src = _row_iota((LANES, GD))
    dst = _lane_iota((LANES, GD))
    place = jnp.where(dst == (src + (g + 1) * HEAD_DIM) % GD, 1.0, 0.0).astype(BF16)
    penalty = _bdot(jnp.where(sel, 0.0, NEG).astype(BF16), place).astype(BF16)
    q_aug = [jnp.where(own, qr, penalty) for qr in q]
    own_k = _lane_iota((tk, GD)) // HEAD_DIM == g
    m_s[...] = jnp.full(m_s.shape, NEG, F32)
    acc_s[...] = jnp.zeros(acc_s.shape, F32)

    ones_grp = (g + 1) % KV_GROUPS

    def with_ones(v):
        return jnp.where(_lane_iota(v.shape) // HEAD_DIM == ones_grp, jnp.ones_like(v), v)

    def normalised(acc):
        den = jnp.sum(jnp.where(_lane_iota(acc.shape) == ones_grp * HEAD_DIM, acc, 0.0), axis=-1, keepdims=True)
        return acc / den

    def flash_step(qs, k, v, bias, ms, accs):
        s = [_bdot_nt(qr, k) for qr in qs]
        if bias is not None:
            s = [sr + bias for sr in s]
        m_new = [jnp.maximum(mo, jnp.max(sr, axis=-1, keepdims=True)) for mo, sr in zip(ms, s)]
        p = [jnp.exp2(sr - mn).astype(BF16) for sr, mn in zip(s, m_new)]
        a = [jnp.exp2(mo - mn) for mo, mn in zip(ms, m_new)]
        pv = [_bdot(pr, v) for pr in p]
        return m_new, [ar * acc + pvr for ar, acc, pvr in zip(a, accs, pv)]

    rows_of = [slice(r * tq_n, (r + 1) * tq_n) for r in heads]

    def key_tile(kt, causal_bias):
        k0 = pl.multiple_of(kt * tk, tk)
        ind = pltpu.bitcast(pltpu.roll(pltpu.bitcast(blk_ref[pl.ds(k0, tk), :], jnp.uint32), g * HEAD_DIM, 1), BF16)
        k_aug = jnp.where(own_k, ks_ref[pl.ds(k0, tk), :], ind)
        m_new, acc_new = flash_step(q_aug, k_aug, with_ones(vs_ref[pl.ds(k0, tk), :]), causal_bias,
                                    [m_s[x, :] for x in rows_of], [acc_s[x, :] for x in rows_of])
        for r, x in enumerate(rows_of):
            acc_s[x, :] = acc_new[r]
            m_s[x, :] = m_new[r]

    n_past = (t0 + tq_n + tk - 1) // tk - 1

    def past_tile(kt, carry):
        key_tile(kt, None)
        return carry

    lax.fori_loop(0, n_past, past_tile, 0)
    key_tile(n_past, jnp.where(n_past * tk + _lane_iota((tq_n, tk)) <= tq, 0.0, NEG))
    o_s = [normalised(acc_s[x, :]) for x in rows_of]

    w0 = pl.multiple_of(jnp.maximum(t0 - WINDOW, 0), tq_n)
    ms = [jnp.full((tq_n, 1), NEG, F32) for _ in heads]
    accs = [jnp.zeros((tq_n, GD), F32) for _ in heads]
    for off in range(0, WINDOW + tq_n, tk):
        width = min(tk, WINDOW + tq_n - off)
        dist = tq - (w0 + off + _lane_iota((tq_n, width)))
        bias = jnp.where(jnp.logical_and(dist >= 0, dist <= WINDOW), 0.0, NEG)
        ms, accs = flash_step(q, kw_ref[pl.ds(w0 + off, width), :], with_ones(vw_ref[pl.ds(w0 + off, width), :]),
                              bias, ms, accs)
    o_w = [normalised(acc) for acc in accs]

    gates = gate_ref[...]
    glane = _lane_iota(gates.shape)

    def gate_col(r, branch):
        return jnp.sum(jnp.where(glane == (g * GROUP_SIZE + r) * 3 + branch, gates, 0.0), axis=-1, keepdims=True)

    o = [gate_col(r, 0) * o_c[r] + gate_col(r, 1) * o_s[r] + gate_col(r, 2) * o_w[r] for r in heads]
    ocat = jnp.concatenate(o, axis=1).astype(BF16)
    src = _row_iota((GROUP_SIZE * GD, GD))
    dst = _lane_iota((GROUP_SIZE * GD, GD))
    move = src == (dst // HEAD_DIM) * GD + g * HEAD_DIM + dst % HEAD_DIM
    o_ref[...] = _bdot(ocat, jnp.where(move, 1.0, 0.0).astype(BF16)).astype(o_ref.dtype)


def attn_prompt(qx, kc, vc, kv16, win16, gates, batch, seq, tq=512, tk=512):
    nq = seq // tq
    nch = seq // CMP_STRIDE
    n_sel = -(-seq // SEL_BLOCK)
    assert n_sel <= LANES and seq % tk == 0 and tk % tq == 0
    ov = _overlap_matrix(nch, n_sel)
    blk = (jnp.arange(seq)[:, None] // SEL_BLOCK == jnp.arange(GD)[None, :] - HEAD_DIM).astype(BF16)
    kv3 = kv16.reshape(batch, seq, D)
    win3 = win16.reshape(batch, seq, 2 * GD)
    once = pl.Buffered(1)
    slab = lambda col: pl.BlockSpec((None, seq, GD), lambda b, i, g: (b, 0, col), pipeline_mode=once)
    cmp_spec = pl.BlockSpec((None, nch, GD), lambda b, i, g: (b, 0, 0), pipeline_mode=once)
    full = lambda a: pl.BlockSpec(a.shape, lambda b, i, g: (0, 0), pipeline_mode=once)
    return pl.pallas_call(
        functools.partial(_attn_prompt_kernel, tk), grid=(batch, nq, KV_GROUPS),
        in_specs=[pl.BlockSpec((tq, GROUP_SIZE * GD), lambda b, i, g: (b * nq + i, g)),
                  cmp_spec, cmp_spec, slab(2), slab(3), slab(0), slab(1), full(blk),
                  pl.BlockSpec((tq, LANES), lambda b, i, g: (b * nq + i, 0)), full(ov)],
        out_specs=pl.BlockSpec((tq, GD), lambda b, i, g: (b * nq + i, g)),
        out_shape=jax.ShapeDtypeStruct((batch * seq, D), BF16),
        scratch_shapes=[pltpu.VMEM((GROUP_SIZE * tq, 1), F32), pltpu.VMEM((GROUP_SIZE * tq, GD), F32)],
        compiler_params=_cparams(3), name="attn_prompt",
    )(qx, kc, vc, kv3, kv3, win3, win3, blk, gates, ov)


def _nsa_sample_kernel(n_pages, past, pt_ref, *refs):
    page_refs = refs[:n_pages]
    (st_ref, qx_ref, kvn_ref, winn_ref, gate_ref, w1k_ref, w1v_ref, pek_ref, pev_ref, w2k_ref, w2v_ref,
     ov_ref, move_ref, o_ref, nwin_ref, acc_s, rows_s) = refs[n_pages:]
    t_new = qx_ref.shape[0]
    rows = N_HEADS * t_new
    wb = st_ref.shape[1]

    acc_s[...] = jnp.zeros(acc_s.shape, F32)
    for i, p in enumerate(page_refs):
        for m in range(2):
            t = jnp.transpose(p[m * GD:(m + 1) * GD, :])
            for half in range(GD // LANES):
                rows_s[m, half, i * PAGE:(i + 1) * PAGE, :] = t[:, half * LANES:(half + 1) * LANES]
    n_chunk = past // CMP_STRIDE
    for l in range(CMP_STRIDE):
        for m, w1_ref in ((0, w1k_ref), (1, w1v_ref)):
            x = jnp.concatenate([rows_s[m, half, pl.ds(l, n_chunk, stride=CMP_STRIDE), :]
                                 for half in range(GD // LANES)], axis=1)
            _cmp_accumulate(acc_s, m, x.astype(BF16), w1_ref, l)
    kc = _cmp_finalize(acc_s, 0, pek_ref, w1k_ref, w2k_ref).astype(BF16)
    vc = _cmp_finalize(acc_s, 1, pev_ref, w1v_ref, w2v_ref).astype(BF16)

    qb = qx_ref[...]
    q = jnp.concatenate([qb[:, h * GD:(h + 1) * GD] for h in range(N_HEADS)], axis=0).astype(BF16)
    qidx = jnp.bitwise_and(_row_iota((rows, 1)), t_new - 1)
    tq = past + qidx

    s = _bdot_nt(q, kc)
    cend = _lane_iota(s.shape) * CMP_STRIDE + (CMP_BLOCK - 1)
    e, den = _masked_softmax(s, cend <= tq)
    p_c = e / jnp.maximum(den, 1e-30)
    o_c = _bdot(p_c.astype(BF16), vc)

    grp = []
    for g in range(KV_GROUPS):
        acc = p_c[g * GROUP_SIZE * t_new:(g * GROUP_SIZE + 1) * t_new]
        for r in range(1, GROUP_SIZE):
            h = g * GROUP_SIZE + r
            acc = acc + p_c[h * t_new:(h + 1) * t_new]
        grp.append(acc)
    p_grp = jnp.concatenate(grp, axis=0)
    sel = _importance_select(p_grp, ov_ref, past + jnp.bitwise_and(_row_iota((KV_GROUPS * t_new, 1)), t_new - 1))
    sel = jnp.where(sel, 1.0, 0.0).astype(BF16)
    sel_h = jnp.concatenate([sel[(h // GROUP_SIZE) * t_new:(h // GROUP_SIZE + 1) * t_new] for h in range(N_HEADS)],
                            axis=0)

    def new_keys(x):
        return jnp.concatenate([x, jnp.zeros((LANES - t_new, x.shape[1]), x.dtype)], axis=0).astype(BF16)

    new_ok = jnp.logical_and(_lane_iota((rows, LANES)) < t_new, _lane_iota((rows, LANES)) <= qidx)

    def two_part_attention(s_old, ok_old, vt_old, s_new, v_new):
        s_old = jnp.where(ok_old, s_old, NEG)
        s_new = jnp.where(new_ok, s_new, NEG)
        m = jnp.maximum(jnp.max(s_old, axis=-1, keepdims=True), jnp.max(s_new, axis=-1, keepdims=True))
        e_old = jnp.where(ok_old, jnp.exp2(s_old - m), 0.0)
        e_new = jnp.where(new_ok, jnp.exp2(s_new - m), 0.0)
        den = jnp.sum(e_old, axis=-1, keepdims=True) + jnp.sum(e_new, axis=-1, keepdims=True)
        return (_bdot_nt(e_old.astype(BF16), vt_old) + _bdot(e_new.astype(BF16), v_new)) / den

    kt_all = jnp.concatenate([p[2 * GD:3 * GD, :] for p in page_refs], axis=1).astype(BF16)
    vt_all = jnp.concatenate([p[3 * GD:4 * GD, :] for p in page_refs], axis=1).astype(BF16)
    kvn = kvn_ref[...]
    expand = _row_iota((LANES, past)) == _lane_iota((LANES, past)) // SEL_BLOCK
    ok_p = _bdot(sel_h, jnp.where(expand, 1.0, 0.0).astype(BF16)) > 0.5
    o_s = two_part_attention(_bdot(q, kt_all), ok_p, vt_all,
                             _bdot_nt(q, new_keys(kvn[:, 2 * GD:3 * GD])), new_keys(kvn[:, 3 * GD:4 * GD]))

    winn = winn_ref[...]
    dist = tq - (past - wb + _lane_iota((rows, wb)))
    ok_w = jnp.logical_and(dist >= 0, dist <= WINDOW)
    o_w = two_part_attention(_bdot(q, st_ref[:GD, :].astype(BF16)), ok_w, st_ref[GD:, :].astype(BF16),
                             _bdot_nt(q, new_keys(winn[:, :GD])), new_keys(winn[:, GD:]))

    gates = gate_ref[...]

    def gate_col(branch):
        return jnp.concatenate([gates[:, h * 3 + branch:h * 3 + branch + 1] for h in range(N_HEADS)], axis=0)

    o = gate_col(0) * o_c + gate_col(1) * o_s + gate_col(2) * o_w
    outs = []
    for g in range(KV_GROUPS):
        ocat = jnp.concatenate([o[(g * GROUP_SIZE + r) * t_new:(g * GROUP_SIZE + r + 1) * t_new]
                                for r in range(GROUP_SIZE)], axis=1).astype(BF16)
        outs.append(_bdot(ocat, move_ref[g]))
    o_ref[...] = jnp.concatenate(outs, axis=1)

    shifted = pltpu.roll(st_ref[...], wb - t_new, 1)
    new_t = jnp.transpose(jnp.concatenate([jnp.zeros((LANES - t_new, 2 * GD), F32), winn], axis=0))
    last = jnp.where(_lane_iota((2 * GD, LANES)) < LANES - t_new, shifted[:, wb - LANES:], new_t)
    nwin_ref[:, :wb - LANES] = shifted[:, :wb - LANES]
    nwin_ref[:, wb - LANES:] = last


def nsa_sample(cache_t, li, page_table, state_win_t, qx_s, kvn, winn, gates_s, cmp_w, past):
    pe_k, w1k, w2k, pe_v, w1v, w2v = cmp_w
    n_seq, n_pages = page_table.shape
    t_new = qx_s.shape[0] // n_seq
    wb = state_win_t.shape[3]
    assert wb >= LANES
    nch = past // CMP_STRIDE
    n_sel = -(-(past + t_new) // SEL_BLOCK)
    ov = _overlap_matrix(nch, n_sel)
    src = jnp.arange(GROUP_SIZE * GD)[None, :, None]
    dst = jnp.arange(GD)[None, None, :]
    grp = jnp.arange(KV_GROUPS)[:, None, None]
    move = (src == (dst // HEAD_DIM) * GD + grp * HEAD_DIM + dst % HEAD_DIM).astype(BF16)
    consts = (w1k, w1v, pe_k.reshape(1, -1), pe_v.reshape(1, -1), _w2_padded(w2k), _w2_padded(w2v), ov, move)
    full = lambda a: pl.BlockSpec(a.shape, lambda b, pt: (0,) * a.ndim)
    seq_rows = lambda w: pl.BlockSpec((t_new, w), lambda b, pt: (b, 0))
    page_spec = lambda p: pl.BlockSpec((None, None, D, PAGE), lambda b, pt: (li, pt[b, p], 0, 0))
    grid_spec = pltpu.PrefetchScalarGridSpec(
        num_scalar_prefetch=1, grid=(n_seq,),
        in_specs=[page_spec(p) for p in range(n_pages)]
        + [pl.BlockSpec((None, None, 2 * GD, wb), lambda b, pt: (li, b, 0, 0)),
           seq_rows(N_HEADS * GD), seq_rows(D), seq_rows(2 * GD), seq_rows(LANES)]
        + [full(a) for a in consts],
        out_specs=[seq_rows(D), pl.BlockSpec((None, 2 * GD, wb), lambda b, pt: (b, 0, 0))],
        scratch_shapes=[pltpu.VMEM((2, KV_GROUPS, nch, 2 * CMP_HIDDEN), F32),
                        pltpu.VMEM((2, GD // LANES, past, LANES), F32)],
    )
    return pl.pallas_call(
        functools.partial(_nsa_sample_kernel, n_pages, past), grid_spec=grid_spec,
        out_shape=[jax.ShapeDtypeStruct((n_seq * t_new, D), F32), jax.ShapeDtypeStruct((n_seq, 2 * GD, wb), F32)],
        compiler_params=_cparams(1), name="nsa_sample",
    )(page_table, *([cache_t] * n_pages), state_win_t, qx_s, kvn, winn, gates_s, *consts)


def _moe_chunks(n):
    chunks = 1
    while n // chunks > 2304 and (n // chunks) % 16 == 0:
        chunks *= 2
    return chunks


def kernel(x_prompt, x_sample, cache_kv, state_win, state_conv, state_h, page_table, p_prompt, p_sample, ln_g, ln_b, rg_w_in, rg_conv_w, rg_conv_b, rg_w_a, rg_b_a, rg_w_i, rg_b_i, rg_lam, rg_w_o, nsa_w_in, nsa_pe_k, nsa_ck_w1, nsa_ck_w2, nsa_pe_v, nsa_cv_w1, nsa_cv_w2, nsa_w_o, moe_w_r, moe_b_r, moe_w_gu, moe_b_gu, moe_w_dn, moe_b_dn, ple_w_proj, ple_w_gate):
    batch, seq = x_prompt.shape[:2]
    n_seq, t_new = x_sample.shape[:2]
    depth = ln_g.shape[0]
    past = page_table.shape[1] * PAGE
    wb = state_win.shape[2]
    assert seq >= wb and wb == min(WINDOW, past)
    alpha = (2 * depth) ** 0.25
    n_p, n_s = batch * seq, n_seq * t_new
    n = n_p + n_s
    x = jnp.concatenate([x_prompt.reshape(n_p, D), x_sample.reshape(n_s, D)], axis=0)
    p_all = jnp.concatenate([p_prompt.reshape(depth, n_p, PLE_DIM), p_sample.reshape(depth, n_s, PLE_DIM)], axis=1)
    pos = jnp.concatenate([jnp.tile(jnp.arange(seq), batch), past + jnp.tile(jnp.arange(t_new), n_seq)])
    cache_t = jnp.transpose(cache_kv, (0, 1, 3, 4, 5, 2)).reshape(cache_kv.shape[0], cache_kv.shape[1], D, PAGE)
    win_t = jnp.transpose(state_win, (0, 1, 3, 4, 5, 2)).reshape(state_win.shape[0], n_seq, 2 * GD, wb)
    w_gu16 = moe_w_gu.astype(BF16)
    w_dn16 = moe_w_dn.astype(BF16)
    n_chunks = _moe_chunks(n)
    kv_p, kv_s, win_p, win_s, conv_p, conv_s, h_p, h_s = [], [], [], [], [], [], [], []
    for i in range(depth):
        li = i // 2
        if i % 2 == 0:
            weights = (rg_w_in[li], rg_conv_w[li], rg_conv_b[li], rg_w_a[li], rg_b_a[li], rg_w_i[li], rg_b_i[li],
                       rg_lam[li])
            hy_p, cp, hp = rg_prompt(x[:n_p], batch, seq, weights)
            hy_s, cs, hs = rg_sample(x[n_p:], n_seq, t_new, state_conv[li], state_h[li], weights)
            conv_p.append(cp)
            conv_s.append(cs)
            h_p.append(hp)
            h_s.append(hs)
            mixed = jnp.concatenate([hy_p, hy_s], axis=0)
            w_o = rg_w_o[li]
        else:
            qx, kv32, win32, kv16, win16, gates = nsa_proj(x, pos, nsa_w_in[li])
            cmp_w = (nsa_pe_k[li], nsa_ck_w1[li], nsa_ck_w2[li], nsa_pe_v[li], nsa_cv_w1[li], nsa_cv_w2[li])
            kc, vc = cmp_prompt(kv16[:n_p], batch, seq, cmp_w)
            o_p = attn_prompt(qx[:n_p], kc, vc, kv16[:n_p], win16[:n_p], gates[:n_p], batch, seq)
            o_s, nwin = nsa_sample(cache_t, li, page_table, win_t, qx[n_p:].astype(F32), kv32[n_p:], win32[n_p:],
                                   gates[n_p:], cmp_w, past)
            kv_p.append(kv32[:n_p].reshape(batch, seq, 4, KV_GROUPS, HEAD_DIM))
            kv_s.append(kv32[n_p:].reshape(n_seq, t_new, 4, KV_GROUPS, HEAD_DIM))
            win_p.append(win32[:n_p].reshape(batch, seq, 2, KV_GROUPS, HEAD_DIM)[:, seq - wb:])
            win_s.append(jnp.transpose(nwin.reshape(n_seq, 2, KV_GROUPS, HEAD_DIM, wb), (0, 4, 1, 2, 3)))
            mixed = jnp.concatenate([o_p, o_s.astype(BF16)], axis=0)
            w_o = nsa_w_o[li]
        x1, xp, route = mix_ln_router(x, mixed, w_o, ln_g[i, 0], ln_b[i, 0], moe_w_r[i], moe_b_r[i], alpha)
        moe = moe_ffn(xp, route, i, w_gu16, moe_b_gu, w_dn16, moe_b_dn, n_chunks)
        x = ln_ple(x1, moe, p_all[i], ln_g[i, 1], ln_b[i, 1], ple_w_gate[i], ple_w_proj[i], alpha)
    return (x[:n_p].reshape(batch, seq, D), x[n_p:].reshape(n_seq, t_new, D),
            jnp.stack(kv_p), jnp.stack(kv_s), jnp.stack(win_p), jnp.stack(win_s),
            jnp.stack(conv_p), jnp.stack(conv_s), jnp.stack(h_p), jnp.stack(h_s))
```

```python
import functools
import math

import jax
import jax.numpy as jnp
from jax import lax
from jax.experimental import pallas as pl
from jax.experimental.pallas import tpu as pltpu

F32 = jnp.float32
BF16 = jnp.bfloat16

D = 1024
N_HEADS = 16
HEAD_DIM = 64
KV_GROUPS = 4
GROUP_SIZE = 4
GD = KV_GROUPS * HEAD_DIM
ROT_HALF = 8
ROPE_THETA = 500000.0
CMP_BLOCK = 32
CMP_STRIDE = 16
CMP_HIDDEN = 128
SEL_BLOCK = 64
SEL_TOP = 16
WINDOW = 512
N_EXPERTS = 32
TOP_K = 4
SWIGLU_LIMIT = 7.0
SWIGLU_ALPHA = 1.702
PLE_DIM = 256
PAGE = 128
RG_C = 8.0
LN_EPS = 1e-5
NEG = -1e30
BIG = 1e30
NSA_PROJ = 2608
NSA_PROJ_PAD = 2688
LANES = 128
MOE_ROWS = 144
VMEM_LIMIT = 56 * 1024 * 1024


def _cparams(n_axes):
    return pltpu.CompilerParams(dimension_semantics=("arbitrary",) * n_axes,
                                vmem_limit_bytes=VMEM_LIMIT)


def _bdot(a, b):
    return jnp.dot(a, b, preferred_element_type=F32)


def _bdot_nt(a, b):
    return lax.dot_general(a, b, (((1,), (1,)), ((), ())), preferred_element_type=F32)


def _gelu(x):
    return 0.5 * x * (1.0 + jnp.tanh(math.sqrt(2.0 / math.pi) * (x + 0.044715 * (x * x * x))))


def _sigmoid(x):
    return 1.0 / (1.0 + jnp.exp(-x))


def _layer_norm(y, g, b):
    mu = jnp.mean(y, axis=-1, keepdims=True)
    yc = y - mu
    var = jnp.mean(yc * yc, axis=-1, keepdims=True)
    return yc * lax.rsqrt(var + LN_EPS) * g + b


def _lane_iota(shape):
    return lax.broadcasted_iota(jnp.int32, shape, len(shape) - 1)


def _row_iota(shape):
    return lax.broadcasted_iota(jnp.int32, shape, 0)


def _top_select(v, n_pick):
    lane = _lane_iota(v.shape).astype(F32)
    sel = jnp.zeros(v.shape, jnp.bool_)
    vals, idxs = [], []
    for _ in range(n_pick):
        m = jnp.max(v, axis=-1, keepdims=True)
        j = jnp.min(jnp.where(v == m, lane, float(v.shape[-1])), axis=-1, keepdims=True)
        pick = lane == j
        sel = jnp.logical_or(sel, pick)
        v = jnp.where(pick, -jnp.inf, v)
        vals.append(m)
        idxs.append(j)
    return sel, vals, idxs


def _mix_ln_router_kernel(alpha, x0_ref, m_ref, wo_ref, g_ref, b_ref, wr_ref, br_ref,
                          x1_ref, xp_ref, route_ref, wo_s, wrh_s, wrl_s):
    @pl.when(pl.program_id(0) == 0)
    def _():
        wo_s[...] = wo_ref[...].astype(BF16)
        wr = wr_ref[...]
        wrh = wr.astype(BF16)
        wrh_s[...] = wrh
        wrl_s[...] = (wr - wrh.astype(F32)).astype(BF16)

    mix = _bdot(m_ref[...].astype(BF16), wo_s[...])
    x1 = _layer_norm(alpha * x0_ref[...] + mix, g_ref[...], b_ref[...])
    x1_ref[...] = x1
    xh = x1.astype(BF16)
    xhf = xh.astype(F32)
    bits = lax.bitcast_convert_type(xhf, jnp.uint32)
    xp_ref[...] = jnp.bitwise_or(bits[:, D // 2:], lax.shift_right_logical(bits[:, :D // 2], jnp.uint32(16)))
    xl = (x1 - xhf).astype(BF16)
    logits = _bdot(xh, wrh_s[...]) + _bdot(xh, wrl_s[...]) + _bdot(xl, wrh_s[...]) + br_ref[...]
    _, vals, idxs = _top_select(logits, TOP_K)
    es = [jnp.exp(v - vals[0]) for v in vals]
    den = es[0] + es[1] + es[2] + es[3]
    lane = _lane_iota(logits.shape)
    route = jnp.zeros(logits.shape, F32)
    for k in range(TOP_K):
        route = jnp.where(lane == k, idxs[k], route)
        route = jnp.where(lane == TOP_K + k, es[k] / den, route)
    route_ref[...] = route


def mix_ln_router(x0, m, w_o, ln_g, ln_b, w_r, b_r, alpha, tm=512):
    n = x0.shape[0]
    wr = jnp.pad(w_r, ((0, 0), (0, LANES - N_EXPERTS)))
    br = jnp.pad(b_r, (0, LANES - N_EXPERTS), constant_values=NEG).reshape(1, LANES)
    row = lambda w: pl.BlockSpec((tm, w), lambda i: (i, 0))
    full = lambda a: pl.BlockSpec(a.shape, lambda i: (0,) * a.ndim)
    g2, b2 = ln_g.reshape(1, D), ln_b.reshape(1, D)
    return pl.pallas_call(
        functools.partial(_mix_ln_router_kernel, alpha),
        grid=(n // tm,),
        in_specs=[row(D), row(D), full(w_o), full(g2), full(b2), full(wr), full(br)],
        out_specs=[row(D), row(D // 2), row(LANES)],
        out_shape=[jax.ShapeDtypeStruct((n, D), F32), jax.ShapeDtypeStruct((n, D // 2), jnp.uint32),
                   jax.ShapeDtypeStruct((n, LANES), F32)],
        scratch_shapes=[pltpu.VMEM((D, D), BF16), pltpu.VMEM((D, LANES), BF16), pltpu.VMEM((D, LANES), BF16)],
        compiler_params=_cparams(1), name="mix_ln_router",
    )(x0, m, w_o, g2, b2, wr, br)


def _moe_kernel(start_ref, count_ref, tok_ref, gate_ref, xp_ref, wgu_ref, bgu_ref, wdn_ref, bdn_ref,
                out_ref, xg_s, y_s):
    c, e = pl.program_id(0), pl.program_id(1)

    @pl.when(jnp.logical_and(c == 0, e == 0))
    def _():
        xg_s[...] = jnp.zeros(xg_s.shape, xg_s.dtype)

    @pl.when(e == 0)
    def _():
        out_ref[...] = jnp.zeros(out_ref.shape, out_ref.dtype)

    start = start_ref[c, e]
    count = count_ref[c, e]

    def block(blk, carry):
        base = start + blk * MOE_ROWS
        rows = jnp.minimum(MOE_ROWS, count - blk * MOE_ROWS)

        groups = (rows + 7) // 8

        def gather(i8, cc):
            i0 = pl.multiple_of(i8 * 8, 8)
            picked = [xp_ref[pl.ds(tok_ref[0, base + i0 + j], 1), :] for j in range(8)]
            xg_s[pl.ds(i0, 8), :] = jnp.concatenate(picked, axis=0)
            return cc

        lax.fori_loop(0, groups, gather, 0)
        u = xg_s[...]
        lo = lax.bitcast_convert_type(lax.shift_left(u, jnp.uint32(16)), F32)
        hi = lax.bitcast_convert_type(jnp.bitwise_and(u, jnp.uint32(0xFFFF0000)), F32)
        xb = jnp.concatenate([lo, hi], axis=1).astype(BF16)
        h = _bdot(xb, wgu_ref[...]) + bgu_ref[...]
        gate = jnp.minimum(h[:, :D], SWIGLU_LIMIT)
        lin = jnp.clip(h[:, D:], -SWIGLU_LIMIT, SWIGLU_LIMIT)
        act = gate * _sigmoid(SWIGLU_ALPHA * gate) * (lin + 1.0)
        y_s[...] = _bdot(act.astype(BF16), wdn_ref[...]) + bdn_ref[...]

        def scatter(i8, cc):
            for j in range(8):
                i = i8 * 8 + j
                t = tok_ref[0, base + i]
                g = jnp.where(i < rows, gate_ref[0, base + i], 0.0)
                out_ref[pl.ds(t, 1), :] = out_ref[pl.ds(t, 1), :] + g * y_s[pl.ds(i, 1), :]
            return cc

        lax.fori_loop(0, groups, scatter, 0)
        return carry

    lax.fori_loop(0, (count + MOE_ROWS - 1) // MOE_ROWS, block, 0)


def moe_ffn(xp, route, layer, w_gu16, b_gu, w_dn16, b_dn, n_chunks):
    n = xp.shape[0]
    ch = n // n_chunks
    n_as = ch * TOP_K
    e_ids = route[:, :TOP_K].astype(jnp.int32).reshape(n_chunks, n_as)
    gates = route[:, TOP_K:2 * TOP_K].reshape(n_chunks, n_as)
    tok = jnp.tile(jnp.arange(ch, dtype=jnp.int32)[:, None], (1, TOP_K)).reshape(1, n_as)
    order = jnp.argsort(e_ids, axis=1, stable=True)
    tok_s = jnp.take_along_axis(jnp.broadcast_to(tok, e_ids.shape), order, axis=1)
    gate_s = jnp.take_along_axis(gates, order, axis=1)
    counts = jnp.sum(e_ids[:, :, None] == jnp.arange(N_EXPERTS, dtype=jnp.int32)[None, None, :], axis=1).astype(jnp.int32)
    starts = (jnp.cumsum(counts, axis=1) - counts).astype(jnp.int32)
    n_pad = n_as + 8
    tok_s = jnp.pad(tok_s, ((0, 0), (0, 8))).reshape(n_chunks, 1, n_pad)
    gate_s = jnp.pad(gate_s, ((0, 0), (0, 8))).reshape(n_chunks, 1, n_pad)
    depth = w_gu16.shape[0]
    grid_spec = pltpu.PrefetchScalarGridSpec(
        num_scalar_prefetch=2,
        grid=(n_chunks, N_EXPERTS),
        in_specs=[
            pl.BlockSpec((None, 1, n_pad), lambda c, e, s, k: (c, 0, 0), memory_space=pltpu.SMEM),
            pl.BlockSpec((None, 1, n_pad), lambda c, e, s, k: (c, 0, 0), memory_space=pltpu.SMEM),
            pl.BlockSpec((ch, D // 2), lambda c, e, s, k: (c, 0)),
            pl.BlockSpec((None, None, D, 2 * D), lambda c, e, s, k: (layer, e, 0, 0)),
            pl.BlockSpec((None, None, 1, 2 * D), lambda c, e, s, k: (layer, e, 0, 0)),
            pl.BlockSpec((None, None, D, D), lambda c, e, s, k: (layer, e, 0, 0)),
            pl.BlockSpec((None, None, 1, D), lambda c, e, s, k: (layer, e, 0, 0)),
        ],
        out_specs=pl.BlockSpec((ch, D), lambda c, e, s, k: (c, 0)),
        scratch_shapes=[pltpu.VMEM((MOE_ROWS, D // 2), jnp.uint32), pltpu.VMEM((MOE_ROWS, D), F32)],
    )
    return pl.pallas_call(
        _moe_kernel, grid_spec=grid_spec,
        out_shape=jax.ShapeDtypeStruct((n, D), F32),
        compiler_params=_cparams(2), name="moe",
    )(starts, counts, tok_s, gate_s, xp, w_gu16, b_gu.reshape(depth, N_EXPERTS, 1, 2 * D), w_dn16,
      b_dn.reshape(depth, N_EXPERTS, 1, D))


def _ln_ple_kernel(alpha, x1_ref, moe_ref, p_ref, g_ref, b_ref, wg_ref, wp_ref, o_ref, wg_s, wp_s):
    @pl.when(pl.program_id(0) == 0)
    def _():
        wg_s[...] = wg_ref[...].astype(BF16)
        wp_s[...] = wp_ref[...].astype(BF16)

    x2 = _layer_norm(alpha * x1_ref[...] + moe_ref[...], g_ref[...], b_ref[...])
    gate = _sigmoid(_bdot(x2.astype(BF16), wg_s[...]))
    o_ref[...] = x2 + gate * _bdot(p_ref[...].astype(BF16), wp_s[...])


def ln_ple(x1, moe, p, ln_g, ln_b, w_gate, w_proj, alpha, tm=512):
    n = x1.shape[0]
    row = lambda w: pl.BlockSpec((tm, w), lambda i: (i, 0))
    full = lambda a: pl.BlockSpec(a.shape, lambda i: (0,) * a.ndim)
    g2, b2 = ln_g.reshape(1, D), ln_b.reshape(1, D)
    return pl.pallas_call(
        functools.partial(_ln_ple_kernel, alpha),
        grid=(n // tm,),
        in_specs=[row(D), row(D), row(PLE_DIM), full(g2), full(b2), full(w_gate), full(w_proj)],
        out_specs=row(D),
        out_shape=jax.ShapeDtypeStruct((n, D), F32),
        scratch_shapes=[pltpu.VMEM((D, D), BF16), pltpu.VMEM((PLE_DIM, D), BF16)],
        compiler_params=_cparams(1), name="ln_ple",
    )(x1, moe, p, g2, b2, w_gate, w_proj)


def _rg_core(seg, x, win_s, cw_ref, cb_ref, wa_s, ba_ref, wi_s, bi_ref, lam_ref, shifted_u, h_prev):
    tm = x.shape[0]
    proj = _bdot(x.astype(BF16), win_s[...])
    y = _gelu(proj[:, :D])
    u = proj[:, D:]
    cw = cw_ref[...]
    conv = cb_ref[...] + cw[3:4] * u
    for k in (1, 2, 3):
        conv = conv + cw[3 - k:4 - k] * shifted_u(u, k)
    nb = wa_s.shape[0]
    bd = D // nb
    ra, ri = [], []
    for n in range(nb):
        cbn = conv[:, n * bd:(n + 1) * bd].astype(BF16)
        ra.append(_bdot(cbn, wa_s[n]))
        ri.append(_bdot(cbn, wi_s[n]))
    r = _sigmoid(jnp.concatenate(ra, axis=1) + ba_ref[...])
    ig = _sigmoid(jnp.concatenate(ri, axis=1) + bi_ref[...])
    z = -lam_ref[...]
    softplus = jnp.maximum(z, 0.0) + jnp.log1p(jnp.exp(-jnp.abs(z)))
    log_a = (-RG_C) * r * softplus
    a = jnp.exp(log_a)
    b = jnp.sqrt(-jnp.tanh(log_a) * (a * a + 1.0)) * ig * conv
    row = _row_iota((tm, D))
    pos = row if seg is None else jnp.bitwise_and(row, seg - 1)
    s = 1
    while s < (tm if seg is None else seg):
        keep = pos >= s
        a_sh = jnp.where(keep, pltpu.roll(a, s, 0), 1.0)
        b_sh = jnp.where(keep, pltpu.roll(b, s, 0), 0.0)
        b = a * b_sh + b
        a = a * a_sh
        s *= 2
    h = b + a * h_prev
    return u, h, (h * y).astype(BF16)


def _rg_cast_weights(win_ref, wa_ref, wi_ref, win_s, wa_s, wi_s):
    win_s[...] = win_ref[...].astype(BF16)
    wa_s[...] = wa_ref[...].astype(BF16)
    wi_s[...] = wi_ref[...].astype(BF16)


def _rg_prompt_kernel(x_ref, win_ref, cw_ref, cb_ref, wa_ref, ba_ref, wi_ref, bi_ref, lam_ref,
                      hy_ref, ul_ref, hl_ref, win_s, wa_s, wi_s, ucar_s, hcar_s):
    b, ti = pl.program_id(0), pl.program_id(1)

    @pl.when(jnp.logical_and(b == 0, ti == 0))
    def _():
        _rg_cast_weights(win_ref, wa_ref, wi_ref, win_s, wa_s, wi_s)

    @pl.when(ti == 0)
    def _():
        ucar_s[...] = jnp.zeros(ucar_s.shape, F32)
        hcar_s[...] = jnp.zeros(hcar_s.shape, F32)

    ucar = ucar_s[...]
    row8 = _row_iota((8, D))

    def shifted_u(u, k):
        rolled = pltpu.roll(u, k, 0)
        head = jnp.where(row8 < k, pltpu.roll(ucar, k, 0), rolled[:8])
        return jnp.concatenate([head, rolled[8:]], axis=0)

    u, h, hy = _rg_core(None, x_ref[...], win_s, cw_ref, cb_ref, wa_s, ba_ref, wi_s, bi_ref, lam_ref,
                        shifted_u, hcar_s[7:8, :])
    hy_ref[...] = hy
    tm = u.shape[0]
    ucar_s[...] = u[tm - 8:]
    hcar_s[...] = h[tm - 8:]
    ul_ref[...] = u[tm - 8:]
    hl_ref[...] = h[tm - 8:]


def _rg_sample_kernel(seg, x_ref, p1_ref, p2_ref, p3_ref, h0_ref, win_ref, cw_ref, cb_ref, wa_ref, ba_ref,
                      wi_ref, bi_ref, lam_ref, hy_ref, u_ref, h_ref, win_s, wa_s, wi_s):
    @pl.when(pl.program_id(0) == 0)
    def _():
        _rg_cast_weights(win_ref, wa_ref, wi_ref, win_s, wa_s, wi_s)

    prevs = (None, p1_ref, p2_ref, p3_ref)
    pos = jnp.bitwise_and(_row_iota(x_ref.shape), seg - 1)

    def shifted_u(u, k):
        return jnp.where(pos >= k, pltpu.roll(u, k, 0), prevs[k][...])

    u, h, hy = _rg_core(seg, x_ref[...], win_s, cw_ref, cb_ref, wa_s, ba_ref, wi_s, bi_ref, lam_ref,
                        shifted_u, h0_ref[...])
    hy_ref[...] = hy
    u_ref[...] = u
    h_ref[...] = h


def _rg_weight_args(w_in, conv_w, conv_b, w_a, b_a, w_i, b_i, lam):
    return (w_in, conv_w, conv_b.reshape(1, D), w_a, b_a.reshape(1, D), w_i, b_i.reshape(1, D), lam.reshape(1, D))


def _rg_scratch(w_a):
    return [pltpu.VMEM((D, 2 * D), BF16), pltpu.VMEM(w_a.shape, BF16), pltpu.VMEM(w_a.shape, BF16)]


def rg_prompt(x, batch, seq, weights, tm=256):
    wargs = _rg_weight_args(*weights)
    nt = seq // tm
    full = lambda a: pl.BlockSpec(a.shape, lambda b, t: (0,) * a.ndim)
    tail = pl.BlockSpec((None, 8, D), lambda b, t: (b, 0, 0))
    hy, ul, hl = pl.pallas_call(
        _rg_prompt_kernel, grid=(batch, nt),
        in_specs=[pl.BlockSpec((tm, D), lambda b, t: (b * nt + t, 0))] + [full(a) for a in wargs],
        out_specs=[pl.BlockSpec((tm, D), lambda b, t: (b * nt + t, 0)), tail, tail],
        out_shape=[jax.ShapeDtypeStruct((batch * seq, D), BF16), jax.ShapeDtypeStruct((batch, 8, D), F32),
                   jax.ShapeDtypeStruct((batch, 8, D), F32)],
        scratch_shapes=_rg_scratch(weights[3]) + [pltpu.VMEM((8, D), F32), pltpu.VMEM((8, D), F32)],
        compiler_params=_cparams(2), name="rg_prompt",
    )(x, *wargs)
    return hy, ul[:, 5:], hl[:, 7]


def rg_sample(x, n_seq, seg, conv_state, h_state, weights, tm=256):
    wargs = _rg_weight_args(*weights)
    n = n_seq * seg
    tm = min(tm, n)
    prevs = []
    for k in (1, 2, 3):
        pk = jnp.zeros((n_seq, seg, D), F32).at[:, :k].set(conv_state[:, 3 - k:])
        prevs.append(pk.reshape(n, D))
    h0 = jnp.broadcast_to(h_state[:, None, :], (n_seq, seg, D)).reshape(n, D)
    row = pl.BlockSpec((tm, D), lambda i: (i, 0))
    full = lambda a: pl.BlockSpec(a.shape, lambda i: (0,) * a.ndim)
    hy, u, h = pl.pallas_call(
        functools.partial(_rg_sample_kernel, seg), grid=(n // tm,),
        in_specs=[row] * 5 + [full(a) for a in wargs],
        out_specs=[row, row, row],
        out_shape=[jax.ShapeDtypeStruct((n, D), BF16), jax.ShapeDtypeStruct((n, D), F32),
                   jax.ShapeDtypeStruct((n, D), F32)],
        scratch_shapes=_rg_scratch(weights[3]),
        compiler_params=_cparams(1), name="rg_sample",
    )(x, *prevs, h0, *wargs)
    u = u.reshape(n_seq, seg, D)
    return hy, u[:, seg - 3:], h.reshape(n_seq, seg, D)[:, seg - 1]


def _rope_tables(pos):
    inv = ROPE_THETA ** (-jnp.arange(ROT_HALF, dtype=F32) / ROT_HALF)
    ang = pos.astype(F32)[:, None] * inv[None, :]
    cos, sin = jnp.cos(ang), jnp.sin(ang)
    n = pos.shape[0]
    rest = HEAD_DIM - 2 * ROT_HALF
    c = jnp.concatenate([cos, cos, jnp.ones((n, rest), F32)], axis=1)
    s1 = jnp.concatenate([-sin, jnp.zeros((n, HEAD_DIM - ROT_HALF), F32)], axis=1)
    s2 = jnp.concatenate([jnp.zeros((n, ROT_HALF), F32), sin, jnp.zeros((n, rest), F32)], axis=1)
    return tuple(jnp.tile(t, (1, LANES // HEAD_DIM)) for t in (c, s1, s2))


def _nsa_proj_kernel(x_ref, w_ref, ct_ref, s1_ref, s2_ref,
                     qx_ref, kv32_ref, win32_ref, kv16_ref, win16_ref, vt_ref, gate_ref, w_s):
    @pl.when(pl.program_id(0) == 0)
    def _():
        w_s[...] = w_ref[...].astype(BF16)

    proj = _bdot(x_ref[...].astype(BF16), w_s[...])
    ct, s1, s2 = ct_ref[...], s1_ref[...], s2_ref[...]

    def rope(blk):
        reps = blk.shape[1] // LANES
        w = blk.shape[1]
        return (blk * jnp.tile(ct, (1, reps)) + pltpu.roll(blk, w - ROT_HALF, 1) * jnp.tile(s1, (1, reps))
                + pltpu.roll(blk, ROT_HALF, 1) * jnp.tile(s2, (1, reps)))

    lane_grp = _lane_iota((x_ref.shape[0], GD)) // HEAD_DIM
    scale = HEAD_DIM ** -0.5 * math.log2(math.e)
    for g in range(KV_GROUPS):
        qg = rope(proj[:, g * GD:(g + 1) * GD]) * scale
        for r in range(GROUP_SIZE):
            moved = qg if r == g else pltpu.roll(qg, ((g - r) % KV_GROUPS) * HEAD_DIM, 1)
            h = g * GROUP_SIZE + r
            qx_ref[:, h * GD:(h + 1) * GD] = jnp.where(lane_grp == g, moved, 0.0).astype(BF16)
    blocks = []
    for i in range(6):
        blk = proj[:, D + i * GD:D + (i + 1) * GD]
        blocks.append(rope(blk) if i % 2 == 0 else blk)
    kv = jnp.concatenate(blocks[:4], axis=1)
    win = jnp.concatenate(blocks[4:], axis=1)
    kv32_ref[...] = kv
    win32_ref[...] = win
    kv16_ref[...] = kv.astype(BF16)
    win16_ref[...] = win.astype(BF16)
    vt_ref[:GD, :] = jnp.transpose(blocks[3]).astype(BF16)
    vt_ref[GD:, :] = jnp.transpose(blocks[5]).astype(BF16)
    gate_ref[...] = _sigmoid(proj[:, D + 6 * GD:])


def nsa_proj(x, pos, w_in, tm=256):
    n = x.shape[0]
    w = jnp.pad(w_in, ((0, 0), (0, NSA_PROJ_PAD - NSA_PROJ)))
    tabs = _rope_tables(pos)
    row = lambda wd: pl.BlockSpec((tm, wd), lambda i: (i, 0))
    return pl.pallas_call(
        _nsa_proj_kernel, grid=(n // tm,),
        in_specs=[row(D), pl.BlockSpec(w.shape, lambda i: (0, 0)), row(LANES), row(LANES), row(LANES)],
        out_specs=[row(N_HEADS * GD), row(D), row(2 * GD), row(D), row(2 * GD),
                   pl.BlockSpec((2 * GD, tm), lambda i: (0, i)), row(LANES)],
        out_shape=[jax.ShapeDtypeStruct((n, N_HEADS * GD), BF16), jax.ShapeDtypeStruct((n, D), F32),
                   jax.ShapeDtypeStruct((n, 2 * GD), F32), jax.ShapeDtypeStruct((n, D), BF16),
                   jax.ShapeDtypeStruct((n, 2 * GD), BF16), jax.ShapeDtypeStruct((2 * GD, n), BF16),
                   jax.ShapeDtypeStruct((n, LANES), F32)],
        scratch_shapes=[pltpu.VMEM((D, NSA_PROJ_PAD), BF16)],
        compiler_params=_cparams(1), name="nsa_proj",
    )(x, w, *tabs)


def _cmp_accumulate(acc_ref, m, x16, w1_ref, l):
    half = (CMP_BLOCK // 2) * HEAD_DIM
    off = l * HEAD_DIM if isinstance(l, int) else pl.multiple_of(l * HEAD_DIM, HEAD_DIM)
    top = w1_ref[pl.ds(off, HEAD_DIM), :]
    bot = w1_ref[pl.ds(half + off, HEAD_DIM), :]
    w = jnp.concatenate([top, bot], axis=1).astype(BF16)
    wrep = jnp.concatenate([w] * KV_GROUPS, axis=0)
    lane_grp = _lane_iota(x16.shape) // HEAD_DIM
    for g in range(KV_GROUPS):
        xg = jnp.where(lane_grp == g, x16, jnp.zeros_like(x16))
        acc_ref[m, g] = acc_ref[m, g] + _bdot(xg, wrep)


def _cmp_finalize(acc_ref, m, pe_ref, w1_ref, w2p_ref):
    nch = acc_ref.shape[2]
    pe8 = jnp.broadcast_to(pe_ref[...], (8, pe_ref.shape[1])).astype(BF16)
    bias = _bdot(pe8, w1_ref[...].astype(BF16))[0:1]
    out = jnp.zeros((nch, GD), F32)
    for g in range(KV_GROUPS):
        a = acc_ref[m, g]
        hid = a[:, :CMP_HIDDEN] + pltpu.roll(a[:, CMP_HIDDEN:], nch - 1, 0) + bias
        out = out + _bdot(_gelu(hid).astype(BF16), w2p_ref[g].astype(BF16))
    return out


def _cmp_prompt_kernel(kv_ref, w1k_ref, w1v_ref, pek_ref, pev_ref, w2k_ref, w2v_ref, kc_ref, vc_ref, acc_s):
    l = pl.program_id(1)

    @pl.when(l == 0)
    def _():
        acc_s[...] = jnp.zeros(acc_s.shape, F32)

    blk = kv_ref[...]
    _cmp_accumulate(acc_s, 0, blk[:, :GD], w1k_ref, l)
    _cmp_accumulate(acc_s, 1, blk[:, GD:2 * GD], w1v_ref, l)

    @pl.when(l == CMP_STRIDE - 1)
    def _():
        kc_ref[...] = _cmp_finalize(acc_s, 0, pek_ref, w1k_ref, w2k_ref).astype(BF16)
        vc_ref[...] = jnp.transpose(_cmp_finalize(acc_s, 1, pev_ref, w1v_ref, w2v_ref)).astype(BF16)


def _w2_padded(w2):
    out = jnp.zeros((KV_GROUPS, CMP_HIDDEN, GD), F32)
    for g in range(KV_GROUPS):
        out = out.at[g, :, g * HEAD_DIM:(g + 1) * HEAD_DIM].set(w2)
    return out


def cmp_prompt(kv16, batch, seq, cmp_w):
    pe_k, w1k, w2k, pe_v, w1v, w2v = cmp_w
    nch = seq // CMP_STRIDE
    kvr = kv16.reshape(batch, nch, CMP_STRIDE * D)
    full = lambda a: pl.BlockSpec(a.shape, lambda b, l: (0,) * a.ndim)
    args = (w1k, w1v, pe_k.reshape(1, -1), pe_v.reshape(1, -1), _w2_padded(w2k), _w2_padded(w2v))
    return pl.pallas_call(
        _cmp_prompt_kernel, grid=(batch, CMP_STRIDE),
        in_specs=[pl.BlockSpec((None, nch, D), lambda b, l: (b, 0, l))] + [full(a) for a in args],
        out_specs=[pl.BlockSpec((None, nch, GD), lambda b, l: (b, 0, 0)),
                   pl.BlockSpec((None, GD, nch), lambda b, l: (b, 0, 0))],
        out_shape=[jax.ShapeDtypeStruct((batch, nch, GD), BF16), jax.ShapeDtypeStruct((batch, GD, nch), BF16)],
        scratch_shapes=[pltpu.VMEM((2, KV_GROUPS, nch, 2 * CMP_HIDDEN), F32)],
        compiler_params=_cparams(2), name="cmp_prompt",
    )(kvr, *args)


def _overlap_matrix(n_c_rows, n_sel):
    c0 = jnp.arange(n_c_rows)[:, None] * CMP_STRIDE
    j0 = jnp.arange(LANES)[None, :] * SEL_BLOCK
    ov = (c0 < j0 + SEL_BLOCK) & (c0 + CMP_BLOCK > j0) & (jnp.arange(LANES)[None, :] < n_sel)
    return ov.astype(BF16)


def _masked_softmax(s, ok):
    s = jnp.where(ok, s, NEG)
    m = jnp.max(s, axis=-1, keepdims=True)
    e = jnp.where(ok, jnp.exp2(s - m), 0.0)
    return e, jnp.sum(e, axis=-1, keepdims=True)


def _importance_select(p_grp, ov_ref, tq):
    ph = p_grp.astype(BF16)
    pl_ = (p_grp - ph.astype(F32)).astype(BF16)
    imp = _bdot(ph, ov_ref[...]) + _bdot(pl_, ov_ref[...])
    jb = _lane_iota(imp.shape)
    jq = tq // SEL_BLOCK
    forced = jnp.logical_or(jb == 0, jnp.logical_or(jb == jq, jb == jq - 1))
    imp = jnp.where(forced, BIG, imp)
    imp = jnp.where(jb > jq, -BIG, imp)
    sel, _, _ = _top_select(imp, SEL_TOP)
    return sel


def _attn_prompt_kernel(tk, qx_ref, kc_ref, vc_ref, ks_ref, vs_ref, kw_ref, vw_ref, blk_ref, gate_ref, ov_ref,
                        o_ref, m_s, acc_s):
    qi, g = pl.program_id(1), pl.program_id(2)
    tq_n = qx_ref.shape[0]
    t0 = qi * tq_n
    heads = range(GROUP_SIZE)
    q = [qx_ref[:, r * GD:(r + 1) * GD] for r in heads]
    tq = t0 + _row_iota((tq_n, 1))

    kc, vc = kc_ref[...], vc_ref[...]
    visible = _lane_iota((tq_n, kc.shape[0])) * CMP_STRIDE + (CMP_BLOCK - 1) <= tq
    s = [_bdot_nt(qr, kc) for qr in q]
    e_den = [_masked_softmax(sr, visible) for sr in s]
    p_c = [e / jnp.maximum(den, 1e-30) for e, den in e_den]
    o_c = [_bdot(pr.astype(BF16), vc) for pr in p_c]
    sel = _importance_select((p_c[0] + p_c[1]) + (p_c[2] + p_c[3]), ov_ref, tq)

    own = _lane_iota((tq_n, GD)) // HEAD_DIM == g
    src = _row_iota((LANES, GD))
    dst = _lane_iota((LANES, GD))
    place = jnp.where(dst == (src + (g + 1) * HEAD_DIM) % GD, 1.0, 0.0).astype(BF16)
    penalty = _bdot(jnp.where(sel, 0.0, NEG).astype(BF16), place).astype(BF16)
    q_aug = [jnp.where(own, qr, penalty) for qr in q]
    own_k = _lane_iota((tk, GD)) // HEAD_DIM == g
    m_s[...] = jnp.full(m_s.shape, NEG, F32)
    acc_s[...] = jnp.zeros(acc_s.shape, F32)

    ones_grp = (g + 1) % KV_GROUPS

    def with_ones(v):
        return jnp.where(_lane_iota(v.shape) // HEAD_DIM == ones_grp, jnp.ones_like(v), v)

    def normalised(acc):
        den = jnp.sum(jnp.where(_lane_iota(acc.shape) == ones_grp * HEAD_DIM, acc, 0.0), axis=-1, keepdims=True)
        return acc / den

    def flash_step(qs, k, v, bias, ms, accs):
        s = [_bdot_nt(qr, k) for qr in qs]
        if bias is not None:
            s = [sr + bias for sr in s]
        m_new = [jnp.maximum(mo, jnp.max(sr, axis=-1, keepdims=True)) for mo, sr in zip(ms, s)]
        p = [jnp.exp2(sr - mn).astype(BF16) for sr, mn in zip(s, m_new)]
        a = [jnp.exp2(mo - mn) for mo, mn in zip(ms, m_new)]
        pv = [_bdot(pr, v) for pr in p]
        return m_new, [ar * acc + pvr for ar, acc, pvr in zip(a, accs, pv)]

    rows_of = [slice(r * tq_n, (r + 1) * tq_n) for r in heads]

    def key_tile(kt, causal_bias):
        k0 = pl.multiple_of(kt * tk, tk)
        ind = pltpu.bitcast(pltpu.roll(pltpu.bitcast(blk_ref[pl.ds(k0, tk), :], jnp.uint32), g * HEAD_DIM, 1), BF16)
        k_aug = jnp.where(own_k, ks_ref[pl.ds(k0, tk), :], ind)
        m_new, acc_new = flash_step(q_aug, k_aug, with_ones(vs_ref[pl.ds(k0, tk), :]), causal_bias,
                                    [m_s[x, :] for x in rows_of], [acc_s[x, :] for x in rows_of])
        for r, x in enumerate(rows_of):
            acc_s[x, :] = acc_new[r]
            m_s[x, :] = m_new[r]

    n_past = (t0 + tq_n + tk - 1) // tk - 1

    def past_tile(kt, carry):
        key_tile(kt, None)
        return carry

    lax.fori_loop(0, n_past, past_tile, 0)
    key_tile(n_past, jnp.where(n_past * tk + _lane_iota((tq_n, tk)) <= tq, 0.0, NEG))
    o_s = [normalised(acc_s[x, :]) for x in rows_of]

    w0 = pl.multiple_of(jnp.maximum(t0 - WINDOW, 0), tq_n)
    ms = [jnp.full((tq_n, 1), NEG, F32) for _ in heads]
    accs = [jnp.zeros((tq_n, GD), F32) for _ in heads]
    for off in range(0, WINDOW + tq_n, tk):
        width = min(tk, WINDOW + tq_n - off)
        dist = tq - (w0 + off + _lane_iota((tq_n, width)))
        bias = jnp.where(jnp.logical_and(dist >= 0, dist <= WINDOW), 0.0, NEG)
        ms, accs = flash_step(q, kw_ref[pl.ds(w0 + off, width), :], with_ones(vw_ref[pl.ds(w0 + off, width), :]),
                              bias, ms, accs)
    o_w = [normalised(acc) for acc in accs]

    gates = gate_ref[...]
    glane = _lane_iota(gates.shape)

    def gate_col(r, branch):
        return jnp.sum(jnp.where(glane == (g * GROUP_SIZE + r) * 3 + branch, gates, 0.0), axis=-1, keepdims=True)

    o = [gate_col(r, 0) * o_c[r] + gate_col(r, 1) * o_s[r] + gate_col(r, 2) * o_w[r] for r in heads]
    ocat = jnp.concatenate(o, axis=1).astype(BF16)
    src = _row_iota((GROUP_SIZE * GD, GD))
    dst = _lane_iota((GROUP_SIZE * GD, GD))
    move = src == (dst // HEAD_DIM) * GD + g * HEAD_DIM + dst % HEAD_DIM
    o_ref[...] = _bdot(ocat, jnp.where(move, 1.0, 0.0).astype(BF16)).astype(o_ref.dtype)


def attn_prompt(qx, kc, vc, kv16, win16, gates, batch, seq, tq=512, tk=512):
    nq = seq // tq
    nch = seq // CMP_STRIDE
    n_sel = -(-seq // SEL_BLOCK)
    assert n_sel <= LANES and seq % tk == 0 and tk % tq == 0
    ov = _overlap_matrix(nch, n_sel)
    blk = (jnp.arange(seq)[:, None] // SEL_BLOCK == jnp.arange(GD)[None, :] - HEAD_DIM).astype(BF16)
    kv3 = kv16.reshape(batch, seq, D)
    win3 = win16.reshape(batch, seq, 2 * GD)
    once = pl.Buffered(1)
    slab = lambda col: pl.BlockSpec((None, seq, GD), lambda b, i, g: (b, 0, col), pipeline_mode=once)
    cmp_spec = pl.BlockSpec((None, nch, GD), lambda b, i, g: (b, 0, 0), pipeline_mode=once)
    full = lambda a: pl.BlockSpec(a.shape, lambda b, i, g: (0, 0), pipeline_mode=once)
    return pl.pallas_call(
        functools.partial(_attn_prompt_kernel, tk), grid=(batch, nq, KV_GROUPS),
        in_specs=[pl.BlockSpec((tq, GROUP_SIZE * GD), lambda b, i, g: (b * nq + i, g)),
                  cmp_spec, cmp_spec, slab(2), slab(3), slab(0), slab(1), full(blk),
                  pl.BlockSpec((tq, LANES), lambda b, i, g: (b * nq + i, 0)), full(ov)],
        out_specs=pl.BlockSpec((tq, GD), lambda b, i, g: (b * nq + i, g)),
        out_shape=jax.ShapeDtypeStruct((batch * seq, D), BF16),
        scratch_shapes=[pltpu.VMEM((GROUP_SIZE * tq, 1), F32), pltpu.VMEM((GROUP_SIZE * tq, GD), F32)],
        compiler_params=_cparams(3), name="attn_prompt",
    )(qx, kc, vc, kv3, kv3, win3, win3, blk, gates, ov)


def _top_select_rows(v, n_pick):
    row = _row_iota(v.shape).astype(F32)
    sel = jnp.zeros(v.shape, jnp.bool_)
    for _ in range(n_pick):
        m = jnp.max(v, axis=0, keepdims=True)
        j = jnp.min(jnp.where(v == m, row, float(v.shape[0])), axis=0, keepdims=True)
        pick = row == j
        sel = jnp.logical_or(sel, pick)
        v = jnp.where(pick, -jnp.inf, v)
    return sel


def _attn_prompt_t_kernel(tk, qx_ref, kc_ref, vct_ref, ks_ref, vst_ref, kw_ref, vwt_ref, blk_ref, gate_ref, ovt_ref,
                          o_ref, qa_s, m_s, acc_s, g_s, o_s):
    qi, g = pl.program_id(1), pl.program_id(2)
    tq_n = qx_ref.shape[0]
    t0 = qi * tq_n
    heads = range(GROUP_SIZE)
    tq = t0 + _lane_iota((1, tq_n))
    qt = [jnp.transpose(qx_ref[:, r * GD:(r + 1) * GD].astype(F32)).astype(BF16) for r in heads]

    kc, vct = kc_ref[...], vct_ref[...]
    visible = _row_iota((kc.shape[0], tq_n)) * CMP_STRIDE + (CMP_BLOCK - 1) <= tq
    s = [jnp.where(visible, _bdot(kc, q), NEG) for q in qt]
    m = [jnp.max(x, axis=0, keepdims=True) for x in s]
    e = [jnp.where(visible, jnp.exp2(x - mx), 0.0) for x, mx in zip(s, m)]
    p_c = [x / jnp.maximum(jnp.sum(x, axis=0, keepdims=True), 1e-30) for x in e]
    o_c = [_bdot(vct, p.astype(BF16)) for p in p_c]
    p_grp = (p_c[0] + p_c[1]) + (p_c[2] + p_c[3])
    ph = p_grp.astype(BF16)
    imp = _bdot(ovt_ref[...], ph) + _bdot(ovt_ref[...], (p_grp - ph.astype(F32)).astype(BF16))
    jb = _row_iota(imp.shape)
    jq = tq // SEL_BLOCK
    forced = jnp.logical_or(jb == 0, jnp.logical_or(jb == jq, jb == jq - 1))
    imp = jnp.where(jb > jq, -BIG, jnp.where(forced, BIG, imp))
    sel = _top_select_rows(imp, SEL_TOP)

    penalty = jnp.where(sel, 0.0, NEG).astype(BF16)
    off1 = pl.multiple_of(((g + 1) % KV_GROUPS) * HEAD_DIM, HEAD_DIM)
    off2 = pl.multiple_of(((g + 2) % KV_GROUPS) * HEAD_DIM, HEAD_DIM)
    for r in heads:
        qa_s[r] = qt[r]
        qa_s[r, pl.ds(off1, HEAD_DIM), :] = penalty[:HEAD_DIM]
        qa_s[r, pl.ds(off2, HEAD_DIM), :] = penalty[HEAD_DIM:]
    q_aug = [qa_s[r] for r in heads]
    own_k = _lane_iota((tk, GD)) // HEAD_DIM == g
    m_s[...] = jnp.full(m_s.shape, NEG, F32)
    acc_s[...] = jnp.zeros(acc_s.shape, F32)

    ones_row = ((g + 1) % KV_GROUPS) * HEAD_DIM

    def with_ones(vt):
        grp = _row_iota(vt.shape) // HEAD_DIM
        return jnp.where(grp == (g + 1) % KV_GROUPS, jnp.ones_like(vt), vt)

    def normalised(acc):
        den = jnp.sum(jnp.where(_row_iota(acc.shape) == ones_row, acc, 0.0), axis=0, keepdims=True)
        return acc / den

    def flash_step(qs, k, vt, bias, ms, accs):
        s = [_bdot(k, q) for q in qs]
        if bias is not None:
            s = [x + bias for x in s]
        m_new = [jnp.maximum(mo, jnp.max(x, axis=0, keepdims=True)) for mo, x in zip(ms, s)]
        p = [jnp.exp2(x - mn).astype(BF16) for x, mn in zip(s, m_new)]
        a = [jnp.exp2(mo - mn) for mo, mn in zip(ms, m_new)]
        pv = [_bdot(vt, pr) for pr in p]
        return m_new, [ar * acc + pvr for ar, acc, pvr in zip(a, accs, pv)]

    def key_tile(kt, causal_bias):
        k0 = pl.multiple_of(kt * tk, tk)
        ind = pltpu.bitcast(pltpu.roll(pltpu.bitcast(blk_ref[pl.ds(k0, tk), :], jnp.uint32), g * HEAD_DIM, 1), BF16)
        k_aug = jnp.where(own_k, ks_ref[pl.ds(k0, tk), :], ind)
        m_new, acc_new = flash_step(q_aug, k_aug, with_ones(vst_ref[:, pl.ds(k0, tk)]), causal_bias,
                                    [m_s[r] for r in heads], [acc_s[r] for r in heads])
        for r in heads:
            acc_s[r] = acc_new[r]
            m_s[r] = m_new[r]

    n_past = (t0 + tq_n + tk - 1) // tk - 1

    def past_tile(kt, carry):
        key_tile(kt, None)
        return carry

    lax.fori_loop(0, n_past, past_tile, 0)
    key_tile(n_past, jnp.where(n_past * tk + _row_iota((tk, tq_n)) <= tq, 0.0, NEG))
    o_sel = [normalised(acc_s[r]) for r in heads]

    w0 = pl.multiple_of(jnp.maximum(t0 - WINDOW, 0), LANES)
    ms = [jnp.full((1, tq_n), NEG, F32) for _ in heads]
    accs = [jnp.zeros((GD, tq_n), F32) for _ in heads]
    for off in range(0, WINDOW + tq_n, tk):
        width = min(tk, WINDOW + tq_n - off)
        dist = tq - (w0 + off + _row_iota((width, tq_n)))
        bias = jnp.where(jnp.logical_and(dist >= 0, dist <= WINDOW), 0.0, NEG)
        ms, accs = flash_step(qt, kw_ref[pl.ds(w0 + off, width), :], with_ones(vwt_ref[:, pl.ds(w0 + off, width)]),
                              bias, ms, accs)
    o_win = [normalised(acc) for acc in accs]

    g_s[...] = jnp.transpose(gate_ref[...])

    def gate_row(r, branch):
        return g_s[pl.ds((g * GROUP_SIZE + r) * 3 + branch, 1), :]

    for r in heads:
        o_s[r] = gate_row(r, 0) * o_c[r] + gate_row(r, 1) * o_sel[r] + gate_row(r, 2) * o_win[r]
    own_rows = pl.ds(pl.multiple_of(g * HEAD_DIM, HEAD_DIM), HEAD_DIM)
    picked = jnp.concatenate([o_s[r, own_rows, :] for r in heads], axis=0)
    o_ref[...] = jnp.transpose(picked).astype(o_ref.dtype)


def attn_prompt_t(qx, kc, vct, kv16, win16, vt_all, gates, batch, seq, tq=512, tk=512):
    nq = seq // tq
    nch = seq // CMP_STRIDE
    n_sel = -(-seq // SEL_BLOCK)
    assert n_sel <= LANES and seq % tk == 0 and tk % tq == 0
    ovt = jnp.transpose(_overlap_matrix(nch, n_sel))
    blk = (jnp.arange(seq)[:, None] // SEL_BLOCK == jnp.arange(GD)[None, :] - HEAD_DIM).astype(BF16)
    kv3 = kv16.reshape(batch, seq, D)
    win3 = win16.reshape(batch, seq, 2 * GD)
    once = pl.Buffered(1)
    slab = lambda col: pl.BlockSpec((None, seq, GD), lambda b, i, g: (b, 0, col), pipeline_mode=once)
    slab_t = lambda row: pl.BlockSpec((GD, seq), lambda b, i, g: (row, b), pipeline_mode=once)
    full = lambda a: pl.BlockSpec(a.shape, lambda b, i, g: (0, 0), pipeline_mode=once)
    return pl.pallas_call(
        functools.partial(_attn_prompt_t_kernel, tk), grid=(batch, nq, KV_GROUPS),
        in_specs=[pl.BlockSpec((tq, GROUP_SIZE * GD), lambda b, i, g: (b * nq + i, g)),
                  pl.BlockSpec((None, nch, GD), lambda b, i, g: (b, 0, 0), pipeline_mode=once),
                  pl.BlockSpec((None, GD, nch), lambda b, i, g: (b, 0, 0), pipeline_mode=once),
                  slab(2), slab_t(0), slab(0), slab_t(1), full(blk),
                  pl.BlockSpec((tq, LANES), lambda b, i, g: (b * nq + i, 0)), full(ovt)],
        out_specs=pl.BlockSpec((tq, GD), lambda b, i, g: (b * nq + i, g)),
        out_shape=jax.ShapeDtypeStruct((batch * seq, D), BF16),
        scratch_shapes=[pltpu.VMEM((GROUP_SIZE, GD, tq), BF16), pltpu.VMEM((GROUP_SIZE, 1, tq), F32),
                        pltpu.VMEM((GROUP_SIZE, GD, tq), F32), pltpu.VMEM((LANES, tq), F32),
                        pltpu.VMEM((GROUP_SIZE, GD, tq), F32)],
        compiler_params=_cparams(3), name="attn_prompt",
    )(qx, kc, vct, kv3, vt_all, win3, vt_all, blk, gates, ovt)


def _nsa_sample_kernel(n_pages, past, pt_ref, *refs):
    page_refs = refs[:n_pages]
    (st_ref, qx_ref, kvn_ref, winn_ref, gate_ref, w1k_ref, w1v_ref, pek_ref, pev_ref, w2k_ref, w2v_ref,
     ov_ref, move_ref, o_ref, nwin_ref, acc_s, rows_s) = refs[n_pages:]
    t_new = qx_ref.shape[0]
    rows = N_HEADS * t_new
    wb = st_ref.shape[1]

    acc_s[...] = jnp.zeros(acc_s.shape, F32)
    for i, p in enumerate(page_refs):
        for m in range(2):
            t = jnp.transpose(p[m * GD:(m + 1) * GD, :])
            for half in range(GD // LANES):
                rows_s[m, half, i * PAGE:(i + 1) * PAGE, :] = t[:, half * LANES:(half + 1) * LANES]
    n_chunk = past // CMP_STRIDE
    for l in range(CMP_STRIDE):
        for m, w1_ref in ((0, w1k_ref), (1, w1v_ref)):
            x = jnp.concatenate([rows_s[m, half, pl.ds(l, n_chunk, stride=CMP_STRIDE), :]
                                 for half in range(GD // LANES)], axis=1)
            _cmp_accumulate(acc_s, m, x.astype(BF16), w1_ref, l)
    kc = _cmp_finalize(acc_s, 0, pek_ref, w1k_ref, w2k_ref).astype(BF16)
    vc = _cmp_finalize(acc_s, 1, pev_ref, w1v_ref, w2v_ref).astype(BF16)

    qb = qx_ref[...]
    q = jnp.concatenate([qb[:, h * GD:(h + 1) * GD] for h in range(N_HEADS)], axis=0).astype(BF16)
    qidx = jnp.bitwise_and(_row_iota((rows, 1)), t_new - 1)
    tq = past + qidx

    s = _bdot_nt(q, kc)
    cend = _lane_iota(s.shape) * CMP_STRIDE + (CMP_BLOCK - 1)
    e, den = _masked_softmax(s, cend <= tq)
    p_c = e / jnp.maximum(den, 1e-30)
    o_c = _bdot(p_c.astype(BF16), vc)

    grp = []
    for g in range(KV_GROUPS):
        acc = p_c[g * GROUP_SIZE * t_new:(g * GROUP_SIZE + 1) * t_new]
        for r in range(1, GROUP_SIZE):
            h = g * GROUP_SIZE + r
            acc = acc + p_c[h * t_new:(h + 1) * t_new]
        grp.append(acc)
    p_grp = jnp.concatenate(grp, axis=0)
    sel = _importance_select(p_grp, ov_ref, past + jnp.bitwise_and(_row_iota((KV_GROUPS * t_new, 1)), t_new - 1))
    sel = jnp.where(sel, 1.0, 0.0).astype(BF16)
    sel_h = jnp.concatenate([sel[(h // GROUP_SIZE) * t_new:(h // GROUP_SIZE + 1) * t_new] for h in range(N_HEADS)],
                            axis=0)

    def new_keys(x):
        return jnp.concatenate([x, jnp.zeros((LANES - t_new, x.shape[1]), x.dtype)], axis=0).astype(BF16)

    new_ok = jnp.logical_and(_lane_iota((rows, LANES)) < t_new, _lane_iota((rows, LANES)) <= qidx)

    def two_part_attention(s_old, ok_old, vt_old, s_new, v_new):
        s_old = jnp.where(ok_old, s_old, NEG)
        s_new = jnp.where(new_ok, s_new, NEG)
        m = jnp.maximum(jnp.max(s_old, axis=-1, keepdims=True), jnp.max(s_new, axis=-1, keepdims=True))
        e_old = jnp.where(ok_old, jnp.exp2(s_old - m), 0.0)
        e_new = jnp.where(new_ok, jnp.exp2(s_new - m), 0.0)
        den = jnp.sum(e_old, axis=-1, keepdims=True) + jnp.sum(e_new, axis=-1, keepdims=True)
        return (_bdot_nt(e_old.astype(BF16), vt_old) + _bdot(e_new.astype(BF16), v_new)) / den

    kt_all = jnp.concatenate([p[2 * GD:3 * GD, :] for p in page_refs], axis=1).astype(BF16)
    vt_all = jnp.concatenate([p[3 * GD:4 * GD, :] for p in page_refs], axis=1).astype(BF16)
    kvn = kvn_ref[...]
    expand = _row_iota((LANES, past)) == _lane_iota((LANES, past)) // SEL_BLOCK
    ok_p = _bdot(sel_h, jnp.where(expand, 1.0, 0.0).astype(BF16)) > 0.5
    o_s = two_part_attention(_bdot(q, kt_all), ok_p, vt_all,
                             _bdot_nt(q, new_keys(kvn[:, 2 * GD:3 * GD])), new_keys(kvn[:, 3 * GD:4 * GD]))

    winn = winn_ref[...]
    dist = tq - (past - wb + _lane_iota((rows, wb)))
    ok_w = jnp.logical_and(dist >= 0, dist <= WINDOW)
    o_w = two_part_attention(_bdot(q, st_ref[:GD, :].astype(BF16)), ok_w, st_ref[GD:, :].astype(BF16),
                             _bdot_nt(q, new_keys(winn[:, :GD])), new_keys(winn[:, GD:]))

    gates = gate_ref[...]

    def gate_col(branch):
        return jnp.concatenate([gates[:, h * 3 + branch:h * 3 + branch + 1] for h in range(N_HEADS)], axis=0)

    o = gate_col(0) * o_c + gate_col(1) * o_s + gate_col(2) * o_w
    outs = []
    for g in range(KV_GROUPS):
        ocat = jnp.concatenate([o[(g * GROUP_SIZE + r) * t_new:(g * GROUP_SIZE + r + 1) * t_new]
                                for r in range(GROUP_SIZE)], axis=1).astype(BF16)
        outs.append(_bdot(ocat, move_ref[g]))
    o_ref[...] = jnp.concatenate(outs, axis=1)

    shifted = pltpu.roll(st_ref[...], wb - t_new, 1)
    new_t = jnp.transpose(jnp.concatenate([jnp.zeros((LANES - t_new, 2 * GD), F32), winn], axis=0))
    last = jnp.where(_lane_iota((2 * GD, LANES)) < LANES - t_new, shifted[:, wb - LANES:], new_t)
    nwin_ref[:, :wb - LANES] = shifted[:, :wb - LANES]
    nwin_ref[:, wb - LANES:] = last


def nsa_sample(cache_t, li, page_table, state_win_t, qx_s, kvn, winn, gates_s, cmp_w, past):
    pe_k, w1k, w2k, pe_v, w1v, w2v = cmp_w
    n_seq, n_pages = page_table.shape
    t_new = qx_s.shape[0] // n_seq
    wb = state_win_t.shape[3]
    assert wb >= LANES
    nch = past // CMP_STRIDE
    n_sel = -(-(past + t_new) // SEL_BLOCK)
    ov = _overlap_matrix(nch, n_sel)
    src = jnp.arange(GROUP_SIZE * GD)[None, :, None]
    dst = jnp.arange(GD)[None, None, :]
    grp = jnp.arange(KV_GROUPS)[:, None, None]
    move = (src == (dst // HEAD_DIM) * GD + grp * HEAD_DIM + dst % HEAD_DIM).astype(BF16)
    consts = (w1k, w1v, pe_k.reshape(1, -1), pe_v.reshape(1, -1), _w2_padded(w2k), _w2_padded(w2v), ov, move)
    full = lambda a: pl.BlockSpec(a.shape, lambda b, pt: (0,) * a.ndim)
    seq_rows = lambda w: pl.BlockSpec((t_new, w), lambda b, pt: (b, 0))
    page_spec = lambda p: pl.BlockSpec((None, None, D, PAGE), lambda b, pt: (li, pt[b, p], 0, 0))
    grid_spec = pltpu.PrefetchScalarGridSpec(
        num_scalar_prefetch=1, grid=(n_seq,),
        in_specs=[page_spec(p) for p in range(n_pages)]
        + [pl.BlockSpec((None, None, 2 * GD, wb), lambda b, pt: (li, b, 0, 0)),
           seq_rows(N_HEADS * GD), seq_rows(D), seq_rows(2 * GD), seq_rows(LANES)]
        + [full(a) for a in consts],
        out_specs=[seq_rows(D), pl.BlockSpec((None, 2 * GD, wb), lambda b, pt: (b, 0, 0))],
        scratch_shapes=[pltpu.VMEM((2, KV_GROUPS, nch, 2 * CMP_HIDDEN), F32),
                        pltpu.VMEM((2, GD // LANES, past, LANES), F32)],
    )
    return pl.pallas_call(
        functools.partial(_nsa_sample_kernel, n_pages, past), grid_spec=grid_spec,
        out_shape=[jax.ShapeDtypeStruct((n_seq * t_new, D), F32), jax.ShapeDtypeStruct((n_seq, 2 * GD, wb), F32)],
        compiler_params=_cparams(1), name="nsa_sample",
    )(page_table, *([cache_t] * n_pages), state_win_t, qx_s, kvn, winn, gates_s, *consts)


def _moe_chunks(n):
    chunks = 1
    while n // chunks > 2304 and (n // chunks) % 16 == 0:
        chunks *= 2
    return chunks


def kernel(x_prompt, x_sample, cache_kv, state_win, state_conv, state_h, page_table, p_prompt, p_sample, ln_g, ln_b, rg_w_in, rg_conv_w, rg_conv_b, rg_w_a, rg_b_a, rg_w_i, rg_b_i, rg_lam, rg_w_o, nsa_w_in, nsa_pe_k, nsa_ck_w1, nsa_ck_w2, nsa_pe_v, nsa_cv_w1, nsa_cv_w2, nsa_w_o, moe_w_r, moe_b_r, moe_w_gu, moe_b_gu, moe_w_dn, moe_b_dn, ple_w_proj, ple_w_gate):
    batch, seq = x_prompt.shape[:2]
    n_seq, t_new = x_sample.shape[:2]
    depth = ln_g.shape[0]
    past = page_table.shape[1] * PAGE
    wb = state_win.shape[2]
    assert seq >= wb and wb == min(WINDOW, past)
    alpha = (2 * depth) ** 0.25
    n_p, n_s = batch * seq, n_seq * t_new
    n = n_p + n_s
    x = jnp.concatenate([x_prompt.reshape(n_p, D), x_sample.reshape(n_s, D)], axis=0)
    p_all = jnp.concatenate([p_prompt.reshape(depth, n_p, PLE_DIM), p_sample.reshape(depth, n_s, PLE_DIM)], axis=1)
    pos = jnp.concatenate([jnp.tile(jnp.arange(seq), batch), past + jnp.tile(jnp.arange(t_new), n_seq)])
    cache_t = jnp.transpose(cache_kv, (0, 1, 3, 4, 5, 2)).reshape(cache_kv.shape[0], cache_kv.shape[1], D, PAGE)
    win_t = jnp.transpose(state_win, (0, 1, 3, 4, 5, 2)).reshape(state_win.shape[0], n_seq, 2 * GD, wb)
    w_gu16 = moe_w_gu.astype(BF16)
    w_dn16 = moe_w_dn.astype(BF16)
    n_chunks = _moe_chunks(n)
    kv_p, kv_s, win_p, win_s, conv_p, conv_s, h_p, h_s = [], [], [], [], [], [], [], []
    for i in range(depth):
        li = i // 2
        if i % 2 == 0:
            weights = (rg_w_in[li], rg_conv_w[li], rg_conv_b[li], rg_w_a[li], rg_b_a[li], rg_w_i[li], rg_b_i[li],
                       rg_lam[li])
            hy_p, cp, hp = rg_prompt(x[:n_p], batch, seq, weights)
            hy_s, cs, hs = rg_sample(x[n_p:], n_seq, t_new, state_conv[li], state_h[li], weights)
            conv_p.append(cp)
            conv_s.append(cs)
            h_p.append(hp)
            h_s.append(hs)
            mixed = jnp.concatenate([hy_p, hy_s], axis=0)
            w_o = rg_w_o[li]
        else:
            qx, kv32, win32, kv16, win16, vt_all, gates = nsa_proj(x, pos, nsa_w_in[li])
            cmp_w = (nsa_pe_k[li], nsa_ck_w1[li], nsa_ck_w2[li], nsa_pe_v[li], nsa_cv_w1[li], nsa_cv_w2[li])
            kc, vct = cmp_prompt(kv16[:n_p], batch, seq, cmp_w)
            o_p = attn_prompt_t(qx[:n_p], kc, vct, kv16[:n_p], win16[:n_p], vt_all, gates[:n_p], batch, seq)
            o_s, nwin = nsa_sample(cache_t, li, page_table, win_t, qx[n_p:].astype(F32), kv32[n_p:], win32[n_p:],
                                   gates[n_p:], cmp_w, past)
            kv_p.append(kv32[:n_p].reshape(batch, seq, 4, KV_GROUPS, HEAD_DIM))
            kv_s.append(kv32[n_p:].reshape(n_seq, t_new, 4, KV_GROUPS, HEAD_DIM))
            win_p.append(win32[:n_p].reshape(batch, seq, 2, KV_GROUPS, HEAD_DIM)[:, seq - wb:])
            win_s.append(jnp.transpose(nwin.reshape(n_seq, 2, KV_GROUPS, HEAD_DIM, wb), (0, 4, 1, 2, 3)))
            mixed = jnp.concatenate([o_p, o_s.astype(BF16)], axis=0)
            w_o = nsa_w_o[li]
        x1, xp, route = mix_ln_router(x, mixed, w_o, ln_g[i, 0], ln_b[i, 0], moe_w_r[i], moe_b_r[i], alpha)
        moe = moe_ffn(xp, route, i, w_gu16, moe_b_gu, w_dn16, moe_b_dn, n_chunks)
        x = ln_ple(x1, moe, p_all[i], ln_g[i, 1], ln_b[i, 1], ple_w_gate[i], ple_w_proj[i], alpha)
    return (x[:n_p].reshape(batch, seq, D), x[n_p:].reshape(n_seq, t_new, D),
            jnp.stack(kv_p), jnp.stack(kv_s), jnp.stack(win_p), jnp.stack(win_s),
            jnp.stack(conv_p), jnp.stack(conv_s), jnp.stack(h_p), jnp.stack(h_s))
```

```python
import functools
import math

import jax
import jax.numpy as jnp
from jax import lax
from jax.experimental import pallas as pl
from jax.experimental.pallas import tpu as pltpu

F32 = jnp.float32
BF16 = jnp.bfloat16

D = 1024
N_HEADS = 16
HEAD_DIM = 64
KV_GROUPS = 4
GROUP_SIZE = 4
GD = KV_GROUPS * HEAD_DIM
ROT_HALF = 8
ROPE_THETA = 500000.0
CMP_BLOCK = 32
CMP_STRIDE = 16
CMP_HIDDEN = 128
SEL_BLOCK = 64
SEL_TOP = 16
WINDOW = 512
N_EXPERTS = 32
TOP_K = 4
SWIGLU_LIMIT = 7.0
SWIGLU_ALPHA = 1.702
PLE_DIM = 256
PAGE = 128
RG_C = 8.0
LN_EPS = 1e-5
NEG = -1e30
BIG = 1e30
NSA_PROJ = 2608
NSA_PROJ_PAD = 2688
LANES = 128
MOE_ROWS = 304
VMEM_LIMIT = 56 * 1024 * 1024


def _cparams(n_axes):
    return pltpu.CompilerParams(dimension_semantics=("arbitrary",) * n_axes,
                                vmem_limit_bytes=VMEM_LIMIT)


def _bdot(a, b):
    return jnp.dot(a, b, preferred_element_type=F32)


def _bdot_nt(a, b):
    return lax.dot_general(a, b, (((1,), (1,)), ((), ())), preferred_element_type=F32)


def _gelu(x):
    return 0.5 * x * (1.0 + jnp.tanh(math.sqrt(2.0 / math.pi) * (x + 0.044715 * (x * x * x))))


def _sigmoid(x):
    return 1.0 / (1.0 + jnp.exp(-x))


def _layer_norm(y, g, b):
    mu = jnp.mean(y, axis=-1, keepdims=True)
    yc = y - mu
    var = jnp.mean(yc * yc, axis=-1, keepdims=True)
    return yc * lax.rsqrt(var + LN_EPS) * g + b


def _lane_iota(shape):
    return lax.broadcasted_iota(jnp.int32, shape, len(shape) - 1)


def _row_iota(shape):
    return lax.broadcasted_iota(jnp.int32, shape, 0)


def _top_select(v, n_pick):
    lane = _lane_iota(v.shape).astype(F32)
    sel = jnp.zeros(v.shape, jnp.bool_)
    vals, idxs = [], []
    for _ in range(n_pick):
        m = jnp.max(v, axis=-1, keepdims=True)
        j = jnp.min(jnp.where(v == m, lane, float(v.shape[-1])), axis=-1, keepdims=True)
        pick = lane == j
        sel = jnp.logical_or(sel, pick)
        v = jnp.where(pick, -jnp.inf, v)
        vals.append(m)
        idxs.append(j)
    return sel, vals, idxs


def _mix_ln_router_kernel(alpha, x0_ref, m_ref, wo_ref, g_ref, b_ref, wr_ref, br_ref,
                          x1_ref, route_ref, wo_s, wrh_s, wrl_s):
    @pl.when(pl.program_id(0) == 0)
    def _():
        wo_s[...] = wo_ref[...].astype(BF16)
        wr = wr_ref[...]
        wrh = wr.astype(BF16)
        wrh_s[...] = wrh
        wrl_s[...] = (wr - wrh.astype(F32)).astype(BF16)

    mix = _bdot(m_ref[...].astype(BF16), wo_s[...])
    x1 = _layer_norm(alpha * x0_ref[...] + mix, g_ref[...], b_ref[...])
    x1_ref[...] = x1
    xh = x1.astype(BF16)
    xl = (x1 - xh.astype(F32)).astype(BF16)
    logits = _bdot(xh, wrh_s[...]) + _bdot(xh, wrl_s[...]) + _bdot(xl, wrh_s[...]) + br_ref[...]
    _, vals, idxs = _top_select(logits, TOP_K)
    es = [jnp.exp(v - vals[0]) for v in vals]
    den = es[0] + es[1] + es[2] + es[3]
    lane = _lane_iota(logits.shape)
    route = jnp.zeros(logits.shape, F32)
    for k in range(TOP_K):
        route = jnp.where(lane == k, idxs[k], route)
        route = jnp.where(lane == TOP_K + k, es[k] / den, route)
    route_ref[...] = route


def mix_ln_router(x0, m, w_o, ln_g, ln_b, w_r, b_r, alpha, tm=512):
    n = x0.shape[0]
    wr = jnp.pad(w_r, ((0, 0), (0, LANES - N_EXPERTS)))
    br = jnp.pad(b_r, (0, LANES - N_EXPERTS), constant_values=NEG).reshape(1, LANES)
    row = lambda w: pl.BlockSpec((tm, w), lambda i: (i, 0))
    full = lambda a: pl.BlockSpec(a.shape, lambda i: (0,) * a.ndim)
    g2, b2 = ln_g.reshape(1, D), ln_b.reshape(1, D)
    return pl.pallas_call(
        functools.partial(_mix_ln_router_kernel, alpha),
        grid=(n // tm,),
        in_specs=[row(D), row(D), full(w_o), full(g2), full(b2), full(wr), full(br)],
        out_specs=[row(D), row(LANES)],
        out_shape=[jax.ShapeDtypeStruct((n, D), F32), jax.ShapeDtypeStruct((n, LANES), F32)],
        scratch_shapes=[pltpu.VMEM((D, D), BF16), pltpu.VMEM((D, LANES), BF16), pltpu.VMEM((D, LANES), BF16)],
        compiler_params=_cparams(1), name="mix_ln_router",
    )(x0, m, w_o, g2, b2, wr, br)


def _moe_kernel(start_ref, count_ref, tok_ref, gate_ref, x_ref, wgu_ref, bgu_ref, wdn_ref, bdn_ref,
                out_ref, xg_s, y_s):
    c, e = pl.program_id(0), pl.program_id(1)

    @pl.when(jnp.logical_and(c == 0, e == 0))
    def _():
        xg_s[...] = jnp.zeros(xg_s.shape, xg_s.dtype)

    @pl.when(e == 0)
    def _():
        out_ref[...] = jnp.zeros(out_ref.shape, out_ref.dtype)

    start = start_ref[c, e]
    count = count_ref[c, e]

    def block(blk, carry):
        base = start + blk * MOE_ROWS
        rows = jnp.minimum(MOE_ROWS, count - blk * MOE_ROWS)

        groups = (rows + 7) // 8

        def gather(i8, cc):
            i0 = pl.multiple_of(i8 * 8, 8)
            picked = [x_ref[pl.ds(tok_ref[0, base + i0 + j], 1), :] for j in range(8)]
            xg_s[pl.ds(i0, 8), :] = jnp.concatenate(picked, axis=0)
            return cc

        lax.fori_loop(0, groups, gather, 0)
        h = _bdot(xg_s[...].astype(BF16), wgu_ref[...]) + bgu_ref[...]
        gate = jnp.minimum(h[:, :D], SWIGLU_LIMIT)
        lin = jnp.clip(h[:, D:], -SWIGLU_LIMIT, SWIGLU_LIMIT)
        act = gate * _sigmoid(SWIGLU_ALPHA * gate) * (lin + 1.0)
        y_s[...] = _bdot(act.astype(BF16), wdn_ref[...]) + bdn_ref[...]

        def scatter(i8, cc):
            for j in range(8):
                i = i8 * 8 + j
                t = tok_ref[0, base + i]
                g = jnp.where(i < rows, gate_ref[0, base + i], 0.0)
                out_ref[pl.ds(t, 1), :] = out_ref[pl.ds(t, 1), :] + g * y_s[pl.ds(i, 1), :]
            return cc

        lax.fori_loop(0, groups, scatter, 0)
        return carry

    lax.fori_loop(0, (count + MOE_ROWS - 1) // MOE_ROWS, block, 0)


def moe_ffn(x, route, layer, w_gu16, b_gu, w_dn16, b_dn, n_chunks):
    n = x.shape[0]
    ch = n // n_chunks
    n_as = ch * TOP_K
    e_ids = route[:, :TOP_K].astype(jnp.int32).reshape(n_chunks, n_as)
    gates = route[:, TOP_K:2 * TOP_K].reshape(n_chunks, n_as)
    tok = jnp.tile(jnp.arange(ch, dtype=jnp.int32)[:, None], (1, TOP_K)).reshape(1, n_as)
    order = jnp.argsort(e_ids, axis=1, stable=True)
    tok_s = jnp.take_along_axis(jnp.broadcast_to(tok, e_ids.shape), order, axis=1)
    gate_s = jnp.take_along_axis(gates, order, axis=1)
    counts = jnp.sum(e_ids[:, :, None] == jnp.arange(N_EXPERTS, dtype=jnp.int32)[None, None, :], axis=1).astype(jnp.int32)
    starts = (jnp.cumsum(counts, axis=1) - counts).astype(jnp.int32)
    n_pad = n_as + 8
    tok_s = jnp.pad(tok_s, ((0, 0), (0, 8))).reshape(n_chunks, 1, n_pad)
    gate_s = jnp.pad(gate_s, ((0, 0), (0, 8))).reshape(n_chunks, 1, n_pad)
    depth = w_gu16.shape[0]
    grid_spec = pltpu.PrefetchScalarGridSpec(
        num_scalar_prefetch=2,
        grid=(n_chunks, N_EXPERTS),
        in_specs=[
            pl.BlockSpec((None, 1, n_pad), lambda c, e, s, k: (c, 0, 0), memory_space=pltpu.SMEM),
            pl.BlockSpec((None, 1, n_pad), lambda c, e, s, k: (c, 0, 0), memory_space=pltpu.SMEM),
            pl.BlockSpec((ch, D), lambda c, e, s, k: (c, 0), pipeline_mode=pl.Buffered(1)),
            pl.BlockSpec((None, None, D, 2 * D), lambda c, e, s, k: (layer, e, 0, 0)),
            pl.BlockSpec((None, None, 1, 2 * D), lambda c, e, s, k: (layer, e, 0, 0)),
            pl.BlockSpec((None, None, D, D), lambda c, e, s, k: (layer, e, 0, 0)),
            pl.BlockSpec((None, None, 1, D), lambda c, e, s, k: (layer, e, 0, 0)),
        ],
        out_specs=pl.BlockSpec((ch, D), lambda c, e, s, k: (c, 0)),
        scratch_shapes=[pltpu.VMEM((MOE_ROWS, D), F32), pltpu.VMEM((MOE_ROWS, D), F32)],
    )
    return pl.pallas_call(
        _moe_kernel, grid_spec=grid_spec,
        out_shape=jax.ShapeDtypeStruct((n, D), F32),
        compiler_params=_cparams(2), name="moe",
    )(starts, counts, tok_s, gate_s, x, w_gu16, b_gu.reshape(depth, N_EXPERTS, 1, 2 * D), w_dn16,
      b_dn.reshape(depth, N_EXPERTS, 1, D))


def _ln_ple_kernel(alpha, x1_ref, moe_ref, p_ref, g_ref, b_ref, wg_ref, wp_ref, o_ref, wg_s, wp_s):
    @pl.when(pl.program_id(0) == 0)
    def _():
        wg_s[...] = wg_ref[...].astype(BF16)
        wp_s[...] = wp_ref[...].astype(BF16)

    x2 = _layer_norm(alpha * x1_ref[...] + moe_ref[...], g_ref[...], b_ref[...])
    gate = _sigmoid(_bdot(x2.astype(BF16), wg_s[...]))
    o_ref[...] = x2 + gate * _bdot(p_ref[...].astype(BF16), wp_s[...])


def ln_ple(x1, moe, p, ln_g, ln_b, w_gate, w_proj, alpha, tm=512):
    n = x1.shape[0]
    row = lambda w: pl.BlockSpec((tm, w), lambda i: (i, 0))
    full = lambda a: pl.BlockSpec(a.shape, lambda i: (0,) * a.ndim)
    g2, b2 = ln_g.reshape(1, D), ln_b.reshape(1, D)
    return pl.pallas_call(
        functools.partial(_ln_ple_kernel, alpha),
        grid=(n // tm,),
        in_specs=[row(D), row(D), row(PLE_DIM), full(g2), full(b2), full(w_gate), full(w_proj)],
        out_specs=row(D),
        out_shape=jax.ShapeDtypeStruct((n, D), F32),
        scratch_shapes=[pltpu.VMEM((D, D), BF16), pltpu.VMEM((PLE_DIM, D), BF16)],
        compiler_params=_cparams(1), name="ln_ple",
    )(x1, moe, p, g2, b2, w_gate, w_proj)


def _rg_core(seg, x, win_s, cw_ref, cb_ref, wa_s, ba_ref, wi_s, bi_ref, lam_ref, shifted_u, h_prev):
    tm = x.shape[0]
    proj = _bdot(x.astype(BF16), win_s[...])
    y = _gelu(proj[:, :D])
    u = proj[:, D:]
    cw = cw_ref[...]
    conv = cb_ref[...] + cw[3:4] * u
    for k in (1, 2, 3):
        conv = conv + cw[3 - k:4 - k] * shifted_u(u, k)
    nb = wa_s.shape[0]
    bd = D // nb
    ra, ri = [], []
    for n in range(nb):
        cbn = conv[:, n * bd:(n + 1) * bd].astype(BF16)
        ra.append(_bdot(cbn, wa_s[n]))
        ri.append(_bdot(cbn, wi_s[n]))
    r = _sigmoid(jnp.concatenate(ra, axis=1) + ba_ref[...])
    ig = _sigmoid(jnp.concatenate(ri, axis=1) + bi_ref[...])
    z = -lam_ref[...]
    softplus = jnp.maximum(z, 0.0) + jnp.log1p(jnp.exp(-jnp.abs(z)))
    log_a = (-RG_C) * r * softplus
    a = jnp.exp(log_a)
    b = jnp.sqrt(-jnp.tanh(log_a) * (a * a + 1.0)) * ig * conv
    row = _row_iota((tm, D))
    pos = row if seg is None else jnp.bitwise_and(row, seg - 1)
    s = 1
    while s < (tm if seg is None else seg):
        keep = pos >= s
        a_sh = jnp.where(keep, pltpu.roll(a, s, 0), 1.0)
        b_sh = jnp.where(keep, pltpu.roll(b, s, 0), 0.0)
        b = a * b_sh + b
        a = a * a_sh
        s *= 2
    h = b + a * h_prev
    return u, h, (h * y).astype(BF16)


def _rg_cast_weights(win_ref, wa_ref, wi_ref, win_s, wa_s, wi_s):
    win_s[...] = win_ref[...].astype(BF16)
    wa_s[...] = wa_ref[...].astype(BF16)
    wi_s[...] = wi_ref[...].astype(BF16)


def _rg_prompt_kernel(x_ref, win_ref, cw_ref, cb_ref, wa_ref, ba_ref, wi_ref, bi_ref, lam_ref,
                      hy_ref, ul_ref, hl_ref, win_s, wa_s, wi_s, ucar_s, hcar_s):
    b, ti = pl.program_id(0), pl.program_id(1)

    @pl.when(jnp.logical_and(b == 0, ti == 0))
    def _():
        _rg_cast_weights(win_ref, wa_ref, wi_ref, win_s, wa_s, wi_s)

    @pl.when(ti == 0)
    def _():
        ucar_s[...] = jnp.zeros(ucar_s.shape, F32)
        hcar_s[...] = jnp.zeros(hcar_s.shape, F32)

    ucar = ucar_s[...]
    row8 = _row_iota((8, D))

    def shifted_u(u, k):
        rolled = pltpu.roll(u, k, 0)
        head = jnp.where(row8 < k, pltpu.roll(ucar, k, 0), rolled[:8])
        return jnp.concatenate([head, rolled[8:]], axis=0)

    u, h, hy = _rg_core(None, x_ref[...], win_s, cw_ref, cb_ref, wa_s, ba_ref, wi_s, bi_ref, lam_ref,
                        shifted_u, hcar_s[7:8, :])
    hy_ref[...] = hy
    tm = u.shape[0]
    ucar_s[...] = u[tm - 8:]
    hcar_s[...] = h[tm - 8:]
    ul_ref[...] = u[tm - 8:]
    hl_ref[...] = h[tm - 8:]


def _rg_sample_kernel(seg, x_ref, p1_ref, p2_ref, p3_ref, h0_ref, win_ref, cw_ref, cb_ref, wa_ref, ba_ref,
                      wi_ref, bi_ref, lam_ref, hy_ref, u_ref, h_ref, win_s, wa_s, wi_s):
    @pl.when(pl.program_id(0) == 0)
    def _():
        _rg_cast_weights(win_ref, wa_ref, wi_ref, win_s, wa_s, wi_s)

    prevs = (None, p1_ref, p2_ref, p3_ref)
    pos = jnp.bitwise_and(_row_iota(x_ref.shape), seg - 1)

    def shifted_u(u, k):
        return jnp.where(pos >= k, pltpu.roll(u, k, 0), prevs[k][...])

    u, h, hy = _rg_core(seg, x_ref[...], win_s, cw_ref, cb_ref, wa_s, ba_ref, wi_s, bi_ref, lam_ref,
                        shifted_u, h0_ref[...])
    hy_ref[...] = hy
    u_ref[...] = u
    h_ref[...] = h


def _rg_weight_args(w_in, conv_w, conv_b, w_a, b_a, w_i, b_i, lam):
    return (w_in, conv_w, conv_b.reshape(1, D), w_a, b_a.reshape(1, D), w_i, b_i.reshape(1, D), lam.reshape(1, D))


def _rg_scratch(w_a):
    return [pltpu.VMEM((D, 2 * D), BF16), pltpu.VMEM(w_a.shape, BF16), pltpu.VMEM(w_a.shape, BF16)]


def rg_prompt(x, batch, seq, weights, tm=256):
    wargs = _rg_weight_args(*weights)
    nt = seq // tm
    full = lambda a: pl.BlockSpec(a.shape, lambda b, t: (0,) * a.ndim)
    tail = pl.BlockSpec((None, 8, D), lambda b, t: (b, 0, 0))
    hy, ul, hl = pl.pallas_call(
        _rg_prompt_kernel, grid=(batch, nt),
        in_specs=[pl.BlockSpec((tm, D), lambda b, t: (b * nt + t, 0))] + [full(a) for a in wargs],
        out_specs=[pl.BlockSpec((tm, D), lambda b, t: (b * nt + t, 0)), tail, tail],
        out_shape=[jax.ShapeDtypeStruct((batch * seq, D), BF16), jax.ShapeDtypeStruct((batch, 8, D), F32),
                   jax.ShapeDtypeStruct((batch, 8, D), F32)],
        scratch_shapes=_rg_scratch(weights[3]) + [pltpu.VMEM((8, D), F32), pltpu.VMEM((8, D), F32)],
        compiler_params=_cparams(2), name="rg_prompt",
    )(x, *wargs)
    return hy, ul[:, 5:], hl[:, 7]


def rg_sample(x, n_seq, seg, conv_state, h_state, weights, tm=256):
    wargs = _rg_weight_args(*weights)
    n = n_seq * seg
    tm = min(tm, n)
    prevs = []
    for k in (1, 2, 3):
        pk = jnp.zeros((n_seq, seg, D), F32).at[:, :k].set(conv_state[:, 3 - k:])
        prevs.append(pk.reshape(n, D))
    h0 = jnp.broadcast_to(h_state[:, None, :], (n_seq, seg, D)).reshape(n, D)
    row = pl.BlockSpec((tm, D), lambda i: (i, 0))
    full = lambda a: pl.BlockSpec(a.shape, lambda i: (0,) * a.ndim)
    hy, u, h = pl.pallas_call(
        functools.partial(_rg_sample_kernel, seg), grid=(n // tm,),
        in_specs=[row] * 5 + [full(a) for a in wargs],
        out_specs=[row, row, row],
        out_shape=[jax.ShapeDtypeStruct((n, D), BF16), jax.ShapeDtypeStruct((n, D), F32),
                   jax.ShapeDtypeStruct((n, D), F32)],
        scratch_shapes=_rg_scratch(weights[3]),
        compiler_params=_cparams(1), name="rg_sample",
    )(x, *prevs, h0, *wargs)
    u = u.reshape(n_seq, seg, D)
    return hy, u[:, seg - 3:], h.reshape(n_seq, seg, D)[:, seg - 1]


def _rope_tables(pos):
    inv = ROPE_THETA ** (-jnp.arange(ROT_HALF, dtype=F32) / ROT_HALF)
    ang = pos.astype(F32)[:, None] * inv[None, :]
    cos, sin = jnp.cos(ang), jnp.sin(ang)
    n = pos.shape[0]
    rest = HEAD_DIM - 2 * ROT_HALF
    c = jnp.concatenate([cos, cos, jnp.ones((n, rest), F32)], axis=1)
    s1 = jnp.concatenate([-sin, jnp.zeros((n, HEAD_DIM - ROT_HALF), F32)], axis=1)
    s2 = jnp.concatenate([jnp.zeros((n, ROT_HALF), F32), sin, jnp.zeros((n, rest), F32)], axis=1)
    return tuple(jnp.tile(t, (1, LANES // HEAD_DIM)) for t in (c, s1, s2))


def _nsa_proj_kernel(x_ref, w_ref, ct_ref, s1_ref, s2_ref,
                     qx_ref, kv32_ref, win32_ref, kv16_ref, win16_ref, vt_ref, gate_ref, w_s):
    @pl.when(pl.program_id(0) == 0)
    def _():
        w_s[...] = w_ref[...].astype(BF16)

    proj = _bdot(x_ref[...].astype(BF16), w_s[...])
    ct, s1, s2 = ct_ref[...], s1_ref[...], s2_ref[...]

    def rope(blk):
        reps = blk.shape[1] // LANES
        w = blk.shape[1]
        return (blk * jnp.tile(ct, (1, reps)) + pltpu.roll(blk, w - ROT_HALF, 1) * jnp.tile(s1, (1, reps))
                + pltpu.roll(blk, ROT_HALF, 1) * jnp.tile(s2, (1, reps)))

    lane_grp = _lane_iota((x_ref.shape[0], GD)) // HEAD_DIM
    scale = HEAD_DIM ** -0.5 * math.log2(math.e)
    for g in range(KV_GROUPS):
        qg = rope(proj[:, g * GD:(g + 1) * GD]) * scale
        for r in range(GROUP_SIZE):
            moved = qg if r == g else pltpu.roll(qg, ((g - r) % KV_GROUPS) * HEAD_DIM, 1)
            h = g * GROUP_SIZE + r
            qx_ref[:, h * GD:(h + 1) * GD] = jnp.where(lane_grp == g, moved, 0.0).astype(BF16)
    blocks = []
    for i in range(6):
        blk = proj[:, D + i * GD:D + (i + 1) * GD]
        blocks.append(rope(blk) if i % 2 == 0 else blk)
    kv = jnp.concatenate(blocks[:4], axis=1)
    win = jnp.concatenate(blocks[4:], axis=1)
    kv32_ref[...] = kv
    win32_ref[...] = win
    kv16_ref[...] = kv.astype(BF16)
    win16_ref[...] = win.astype(BF16)
    vt_ref[:GD, :] = jnp.transpose(blocks[3]).astype(BF16)
    vt_ref[GD:, :] = jnp.transpose(blocks[5]).astype(BF16)
    gate_ref[...] = _sigmoid(proj[:, D + 6 * GD:])


def nsa_proj(x, pos, w_in, tm=256):
    n = x.shape[0]
    w = jnp.pad(w_in, ((0, 0), (0, NSA_PROJ_PAD - NSA_PROJ)))
    tabs = _rope_tables(pos)
    row = lambda wd: pl.BlockSpec((tm, wd), lambda i: (i, 0))
    return pl.pallas_call(
        _nsa_proj_kernel, grid=(n // tm,),
        in_specs=[row(D), pl.BlockSpec(w.shape, lambda i: (0, 0)), row(LANES), row(LANES), row(LANES)],
        out_specs=[row(N_HEADS * GD), row(D), row(2 * GD), row(D), row(2 * GD),
                   pl.BlockSpec((2 * GD, tm), lambda i: (0, i)), row(LANES)],
        out_shape=[jax.ShapeDtypeStruct((n, N_HEADS * GD), BF16), jax.ShapeDtypeStruct((n, D), F32),
                   jax.ShapeDtypeStruct((n, 2 * GD), F32), jax.ShapeDtypeStruct((n, D), BF16),
                   jax.ShapeDtypeStruct((n, 2 * GD), BF16), jax.ShapeDtypeStruct((2 * GD, n), BF16),
                   jax.ShapeDtypeStruct((n, LANES), F32)],
        scratch_shapes=[pltpu.VMEM((D, NSA_PROJ_PAD), BF16)],
        compiler_params=_cparams(1), name="nsa_proj",
    )(x, w, *tabs)


def _cmp_accumulate(acc_ref, m, x16, w1_ref, l):
    half = (CMP_BLOCK // 2) * HEAD_DIM
    off = l * HEAD_DIM if isinstance(l, int) else pl.multiple_of(l * HEAD_DIM, HEAD_DIM)
    top = w1_ref[pl.ds(off, HEAD_DIM), :]
    bot = w1_ref[pl.ds(half + off, HEAD_DIM), :]
    w = jnp.concatenate([top, bot], axis=1).astype(BF16)
    wrep = jnp.concatenate([w] * KV_GROUPS, axis=0)
    lane_grp = _lane_iota(x16.shape) // HEAD_DIM
    for g in range(KV_GROUPS):
        xg = jnp.where(lane_grp == g, x16, jnp.zeros_like(x16))
        acc_ref[m, g] = acc_ref[m, g] + _bdot(xg, wrep)


def _cmp_finalize(acc_ref, m, pe_ref, w1_ref, w2p_ref):
    nch = acc_ref.shape[2]
    pe8 = jnp.broadcast_to(pe_ref[...], (8, pe_ref.shape[1])).astype(BF16)
    bias = _bdot(pe8, w1_ref[...].astype(BF16))[0:1]
    out = jnp.zeros((nch, GD), F32)
    for g in range(KV_GROUPS):
        a = acc_ref[m, g]
        hid = a[:, :CMP_HIDDEN] + pltpu.roll(a[:, CMP_HIDDEN:], nch - 1, 0) + bias
        out = out + _bdot(_gelu(hid).astype(BF16), w2p_ref[g].astype(BF16))
    return out


def _cmp_prompt_kernel(kv_ref, w1k_ref, w1v_ref, pek_ref, pev_ref, w2k_ref, w2v_ref, kc_ref, vc_ref, acc_s):
    l = pl.program_id(1)

    @pl.when(l == 0)
    def _():
        acc_s[...] = jnp.zeros(acc_s.shape, F32)

    blk = kv_ref[...]
    _cmp_accumulate(acc_s, 0, blk[:, :GD], w1k_ref, l)
    _cmp_accumulate(acc_s, 1, blk[:, GD:2 * GD], w1v_ref, l)

    @pl.when(l == CMP_STRIDE - 1)
    def _():
        kc_ref[...] = _cmp_finalize(acc_s, 0, pek_ref, w1k_ref, w2k_ref).astype(BF16)
        vc_ref[...] = jnp.transpose(_cmp_finalize(acc_s, 1, pev_ref, w1v_ref, w2v_ref)).astype(BF16)


def _w2_padded(w2):
    out = jnp.zeros((KV_GROUPS, CMP_HIDDEN, GD), F32)
    for g in range(KV_GROUPS):
        out = out.at[g, :, g * HEAD_DIM:(g + 1) * HEAD_DIM].set(w2)
    return out


def cmp_prompt(kv16, batch, seq, cmp_w):
    pe_k, w1k, w2k, pe_v, w1v, w2v = cmp_w
    nch = seq // CMP_STRIDE
    kvr = kv16.reshape(batch, nch, CMP_STRIDE * D)
    full = lambda a: pl.BlockSpec(a.shape, lambda b, l: (0,) * a.ndim)
    args = (w1k, w1v, pe_k.reshape(1, -1), pe_v.reshape(1, -1), _w2_padded(w2k), _w2_padded(w2v))
    return pl.pallas_call(
        _cmp_prompt_kernel, grid=(batch, CMP_STRIDE),
        in_specs=[pl.BlockSpec((None, nch, D), lambda b, l: (b, 0, l))] + [full(a) for a in args],
        out_specs=[pl.BlockSpec((None, nch, GD), lambda b, l: (b, 0, 0)),
                   pl.BlockSpec((None, GD, nch), lambda b, l: (b, 0, 0))],
        out_shape=[jax.ShapeDtypeStruct((batch, nch, GD), BF16), jax.ShapeDtypeStruct((batch, GD, nch), BF16)],
        scratch_shapes=[pltpu.VMEM((2, KV_GROUPS, nch, 2 * CMP_HIDDEN), F32)],
        compiler_params=_cparams(2), name="cmp_prompt",
    )(kvr, *args)


def _overlap_matrix(n_c_rows, n_sel):
    c0 = jnp.arange(n_c_rows)[:, None] * CMP_STRIDE
    j0 = jnp.arange(LANES)[None, :] * SEL_BLOCK
    ov = (c0 < j0 + SEL_BLOCK) & (c0 + CMP_BLOCK > j0) & (jnp.arange(LANES)[None, :] < n_sel)
    return ov.astype(BF16)


def _masked_softmax(s, ok):
    s = jnp.where(ok, s, NEG)
    m = jnp.max(s, axis=-1, keepdims=True)
    e = jnp.where(ok, jnp.exp2(s - m), 0.0)
    return e, jnp.sum(e, axis=-1, keepdims=True)


def _importance_select(p_grp, ov_ref, tq):
    ph = p_grp.astype(BF16)
    pl_ = (p_grp - ph.astype(F32)).astype(BF16)
    imp = _bdot(ph, ov_ref[...]) + _bdot(pl_, ov_ref[...])
    jb = _lane_iota(imp.shape)
    jq = tq // SEL_BLOCK
    forced = jnp.logical_or(jb == 0, jnp.logical_or(jb == jq, jb == jq - 1))
    imp = jnp.where(forced, BIG, imp)
    imp = jnp.where(jb > jq, -BIG, imp)
    sel, _, _ = _top_select(imp, SEL_TOP)
    return sel


def _top_select_rows(v, n_pick):
    row = _row_iota(v.shape).astype(F32)
    sel = jnp.zeros(v.shape, jnp.bool_)
    for _ in range(n_pick):
        m = jnp.max(v, axis=0, keepdims=True)
        j = jnp.min(jnp.where(v == m, row, float(v.shape[0])), axis=0, keepdims=True)
        pick = row == j
        sel = jnp.logical_or(sel, pick)
        v = jnp.where(pick, -jnp.inf, v)
    return sel


def _attn_prompt_t_kernel(tk, qx_ref, kc_ref, vct_ref, ks_ref, vst_ref, kw_ref, vwt_ref, blk_ref, gate_ref, ovt_ref,
                          o_ref, qa_s, m_s, acc_s, g_s, o_s):
    qi, g = pl.program_id(1), pl.program_id(2)
    tq_n = qx_ref.shape[0]
    t0 = qi * tq_n
    heads = range(GROUP_SIZE)
    tq = t0 + _lane_iota((1, tq_n))
    qt = [jnp.transpose(qx_ref[:, r * GD:(r + 1) * GD].astype(F32)).astype(BF16) for r in heads]

    kc, vct = kc_ref[...], vct_ref[...]
    visible = _row_iota((kc.shape[0], tq_n)) * CMP_STRIDE + (CMP_BLOCK - 1) <= tq
    s = [jnp.where(visible, _bdot(kc, q), NEG) for q in qt]
    m = [jnp.max(x, axis=0, keepdims=True) for x in s]
    e = [jnp.where(visible, jnp.exp2(x - mx), 0.0) for x, mx in zip(s, m)]
    p_c = [x / jnp.maximum(jnp.sum(x, axis=0, keepdims=True), 1e-30) for x in e]
    o_c = [_bdot(vct, p.astype(BF16)) for p in p_c]
    p_grp = (p_c[0] + p_c[1]) + (p_c[2] + p_c[3])
    ph = p_grp.astype(BF16)
    imp = _bdot(ovt_ref[...], ph) + _bdot(ovt_ref[...], (p_grp - ph.astype(F32)).astype(BF16))
    jb = _row_iota(imp.shape)
    jq = tq // SEL_BLOCK
    forced = jnp.logical_or(jb == 0, jnp.logical_or(jb == jq, jb == jq - 1))
    imp = jnp.where(jb > jq, -BIG, jnp.where(forced, BIG, imp))
    sel = _top_select_rows(imp, SEL_TOP)

    penalty = jnp.where(sel, 0.0, NEG).astype(BF16)
    off1 = pl.multiple_of(((g + 1) % KV_GROUPS) * HEAD_DIM, HEAD_DIM)
    off2 = pl.multiple_of(((g + 2) % KV_GROUPS) * HEAD_DIM, HEAD_DIM)
    for r in heads:
        qa_s[r] = qt[r]
        qa_s[r, pl.ds(off1, HEAD_DIM), :] = penalty[:HEAD_DIM]
        qa_s[r, pl.ds(off2, HEAD_DIM), :] = penalty[HEAD_DIM:]
    q_aug = [qa_s[r] for r in heads]
    own_k = _lane_iota((tk, GD)) // HEAD_DIM == g
    m_s[...] = jnp.full(m_s.shape, NEG, F32)
    acc_s[...] = jnp.zeros(acc_s.shape, F32)

    ones_row = ((g + 1) % KV_GROUPS) * HEAD_DIM

    def with_ones(vt):
        grp = _row_iota(vt.shape) // HEAD_DIM
        return jnp.where(grp == (g + 1) % KV_GROUPS, jnp.ones_like(vt), vt)

    def normalised(acc):
        den = jnp.sum(jnp.where(_row_iota(acc.shape) == ones_row, acc, 0.0), axis=0, keepdims=True)
        return acc / den

    def flash_step(qs, k, vt, bias, ms, accs):
        s = [_bdot(k, q) for q in qs]
        if bias is not None:
            s = [x + bias for x in s]
        m_new = [jnp.maximum(mo, jnp.max(x, axis=0, keepdims=True)) for mo, x in zip(ms, s)]
        p = [jnp.exp2(x - mn).astype(BF16) for x, mn in zip(s, m_new)]
        a = [jnp.exp2(mo - mn) for mo, mn in zip(ms, m_new)]
        pv = [_bdot(vt, pr) for pr in p]
        return m_new, [ar * acc + pvr for ar, acc, pvr in zip(a, accs, pv)]

    def key_tile(kt, causal_bias):
        k0 = pl.multiple_of(kt * tk, tk)
        ind = pltpu.bitcast(pltpu.roll(pltpu.bitcast(blk_ref[pl.ds(k0, tk), :], jnp.uint32), g * HEAD_DIM, 1), BF16)
        k_aug = jnp.where(own_k, ks_ref[pl.ds(k0, tk), :], ind)
        m_new, acc_new = flash_step(q_aug, k_aug, with_ones(vst_ref[:, pl.ds(k0, tk)]), causal_bias,
                                    [m_s[r] for r in heads], [acc_s[r] for r in heads])
        for r in heads:
            acc_s[r] = acc_new[r]
            m_s[r] = m_new[r]

    n_past = (t0 + tq_n + tk - 1) // tk - 1

    def past_tile(kt, carry):
        key_tile(kt, None)
        return carry

    lax.fori_loop(0, n_past, past_tile, 0)
    key_tile(n_past, jnp.where(n_past * tk + _row_iota((tk, tq_n)) <= tq, 0.0, NEG))
    o_sel = [normalised(acc_s[r]) for r in heads]

    w0 = pl.multiple_of(jnp.maximum(t0 - WINDOW, 0), LANES)
    ms = [jnp.full((1, tq_n), NEG, F32) for _ in heads]
    accs = [jnp.zeros((GD, tq_n), F32) for _ in heads]
    for off in range(0, WINDOW + tq_n, tk):
        width = min(tk, WINDOW + tq_n - off)
        dist = tq - (w0 + off + _row_iota((width, tq_n)))
        bias = jnp.where(jnp.logical_and(dist >= 0, dist <= WINDOW), 0.0, NEG)
        ms, accs = flash_step(qt, kw_ref[pl.ds(w0 + off, width), :], with_ones(vwt_ref[:, pl.ds(w0 + off, width)]),
                              bias, ms, accs)
    o_win = [normalised(acc) for acc in accs]

    g_s[...] = jnp.transpose(gate_ref[...])

    def gate_row(r, branch):
        return g_s[pl.ds((g * GROUP_SIZE + r) * 3 + branch, 1), :]

    for r in heads:
        o_s[r] = gate_row(r, 0) * o_c[r] + gate_row(r, 1) * o_sel[r] + gate_row(r, 2) * o_win[r]
    own_rows = pl.ds(pl.multiple_of(g * HEAD_DIM, HEAD_DIM), HEAD_DIM)
    picked = jnp.concatenate([o_s[r, own_rows, :] for r in heads], axis=0)
    o_ref[...] = jnp.transpose(picked).astype(o_ref.dtype)


def attn_prompt_t(qx, kc, vct, kv16, win16, vt_all, gates, batch, seq, tq=512, tk=512):
    nq = seq // tq
    nch = seq // CMP_STRIDE
    n_sel = -(-seq // SEL_BLOCK)
    assert n_sel <= LANES and seq % tk == 0 and tk % tq == 0
    ovt = jnp.transpose(_overlap_matrix(nch, n_sel))
    blk = (jnp.arange(seq)[:, None] // SEL_BLOCK == jnp.arange(GD)[None, :] - HEAD_DIM).astype(BF16)
    kv3 = kv16.reshape(batch, seq, D)
    win3 = win16.reshape(batch, seq, 2 * GD)
    once = pl.Buffered(1)
    slab = lambda col: pl.BlockSpec((None, seq, GD), lambda b, i, g: (b, 0, col), pipeline_mode=once)
    slab_t = lambda row: pl.BlockSpec((GD, seq), lambda b, i, g: (row, b), pipeline_mode=once)
    full = lambda a: pl.BlockSpec(a.shape, lambda b, i, g: (0, 0), pipeline_mode=once)
    return pl.pallas_call(
        functools.partial(_attn_prompt_t_kernel, tk), grid=(batch, nq, KV_GROUPS),
        in_specs=[pl.BlockSpec((tq, GROUP_SIZE * GD), lambda b, i, g: (b * nq + i, g)),
                  pl.BlockSpec((None, nch, GD), lambda b, i, g: (b, 0, 0), pipeline_mode=once),
                  pl.BlockSpec((None, GD, nch), lambda b, i, g: (b, 0, 0), pipeline_mode=once),
                  slab(2), slab_t(0), slab(0), slab_t(1), full(blk),
                  pl.BlockSpec((tq, LANES), lambda b, i, g: (b * nq + i, 0)), full(ovt)],
        out_specs=pl.BlockSpec((tq, GD), lambda b, i, g: (b * nq + i, g)),
        out_shape=jax.ShapeDtypeStruct((batch * seq, D), BF16),
        scratch_shapes=[pltpu.VMEM((GROUP_SIZE, GD, tq), BF16), pltpu.VMEM((GROUP_SIZE, 1, tq), F32),
                        pltpu.VMEM((GROUP_SIZE, GD, tq), F32), pltpu.VMEM((LANES, tq), F32),
                        pltpu.VMEM((GROUP_SIZE, GD, tq), F32)],
        compiler_params=_cparams(3), name="attn_prompt",
    )(qx, kc, vct, kv3, vt_all, win3, vt_all, blk, gates, ovt)


def _nsa_sample_kernel(n_pages, past, pt_ref, *refs):
    page_refs = refs[:n_pages]
    (st_ref, qx_ref, kvn_ref, winn_ref, gate_ref, w1k_ref, w1v_ref, pek_ref, pev_ref, w2k_ref, w2v_ref,
     ov_ref, move_ref, o_ref, nwin_ref, acc_s, rows_s) = refs[n_pages:]
    t_new = qx_ref.shape[0]
    rows = N_HEADS * t_new
    wb = st_ref.shape[1]

    acc_s[...] = jnp.zeros(acc_s.shape, F32)
    for i, p in enumerate(page_refs):
        for m in range(2):
            t = jnp.transpose(p[m * GD:(m + 1) * GD, :])
            for half in range(GD // LANES):
                rows_s[m, half, i * PAGE:(i + 1) * PAGE, :] = t[:, half * LANES:(half + 1) * LANES]
    n_chunk = past // CMP_STRIDE
    for l in range(CMP_STRIDE):
        for m, w1_ref in ((0, w1k_ref), (1, w1v_ref)):
            x = jnp.concatenate([rows_s[m, half, pl.ds(l, n_chunk, stride=CMP_STRIDE), :]
                                 for half in range(GD // LANES)], axis=1)
            _cmp_accumulate(acc_s, m, x.astype(BF16), w1_ref, l)
    kc = _cmp_finalize(acc_s, 0, pek_ref, w1k_ref, w2k_ref).astype(BF16)
    vc = _cmp_finalize(acc_s, 1, pev_ref, w1v_ref, w2v_ref).astype(BF16)

    qb = qx_ref[...]
    q = jnp.concatenate([qb[:, h * GD:(h + 1) * GD] for h in range(N_HEADS)], axis=0).astype(BF16)
    qidx = jnp.bitwise_and(_row_iota((rows, 1)), t_new - 1)
    tq = past + qidx

    s = _bdot_nt(q, kc)
    cend = _lane_iota(s.shape) * CMP_STRIDE + (CMP_BLOCK - 1)
    e, den = _masked_softmax(s, cend <= tq)
    p_c = e / jnp.maximum(den, 1e-30)
    o_c = _bdot(p_c.astype(BF16), vc)

    grp = []
    for g in range(KV_GROUPS):
        acc = p_c[g * GROUP_SIZE * t_new:(g * GROUP_SIZE + 1) * t_new]
        for r in range(1, GROUP_SIZE):
            h = g * GROUP_SIZE + r
            acc = acc + p_c[h * t_new:(h + 1) * t_new]
        grp.append(acc)
    p_grp = jnp.concatenate(grp, axis=0)
    sel = _importance_select(p_grp, ov_ref, past + jnp.bitwise_and(_row_iota((KV_GROUPS * t_new, 1)), t_new - 1))
    sel = jnp.where(sel, 1.0, 0.0).astype(BF16)
    sel_h = jnp.concatenate([sel[(h // GROUP_SIZE) * t_new:(h // GROUP_SIZE + 1) * t_new] for h in range(N_HEADS)],
                            axis=0)

    def new_keys(x):
        return jnp.concatenate([x, jnp.zeros((LANES - t_new, x.shape[1]), x.dtype)], axis=0).astype(BF16)

    new_ok = jnp.logical_and(_lane_iota((rows, LANES)) < t_new, _lane_iota((rows, LANES)) <= qidx)

    def two_part_attention(s_old, ok_old, vt_old, s_new, v_new):
        s_old = jnp.where(ok_old, s_old, NEG)
        s_new = jnp.where(new_ok, s_new, NEG)
        m = jnp.maximum(jnp.max(s_old, axis=-1, keepdims=True), jnp.max(s_new, axis=-1, keepdims=True))
        e_old = jnp.where(ok_old, jnp.exp2(s_old - m), 0.0)
        e_new = jnp.where(new_ok, jnp.exp2(s_new - m), 0.0)
        den = jnp.sum(e_old, axis=-1, keepdims=True) + jnp.sum(e_new, axis=-1, keepdims=True)
        return (_bdot_nt(e_old.astype(BF16), vt_old) + _bdot(e_new.astype(BF16), v_new)) / den

    kt_all = jnp.concatenate([p[2 * GD:3 * GD, :] for p in page_refs], axis=1).astype(BF16)
    vt_all = jnp.concatenate([p[3 * GD:4 * GD, :] for p in page_refs], axis=1).astype(BF16)
    kvn = kvn_ref[...]
    expand = _row_iota((LANES, past)) == _lane_iota((LANES, past)) // SEL_BLOCK
    ok_p = _bdot(sel_h, jnp.where(expand, 1.0, 0.0).astype(BF16)) > 0.5
    o_s = two_part_attention(_bdot(q, kt_all), ok_p, vt_all,
                             _bdot_nt(q, new_keys(kvn[:, 2 * GD:3 * GD])), new_keys(kvn[:, 3 * GD:4 * GD]))

    winn = winn_ref[...]
    dist = tq - (past - wb + _lane_iota((rows, wb)))
    ok_w = jnp.logical_and(dist >= 0, dist <= WINDOW)
    o_w = two_part_attention(_bdot(q, st_ref[:GD, :].astype(BF16)), ok_w, st_ref[GD:, :].astype(BF16),
                             _bdot_nt(q, new_keys(winn[:, :GD])), new_keys(winn[:, GD:]))

    gates = gate_ref[...]

    def gate_col(branch):
        return jnp.concatenate([gates[:, h * 3 + branch:h * 3 + branch + 1] for h in range(N_HEADS)], axis=0)

    o = gate_col(0) * o_c + gate_col(1) * o_s + gate_col(2) * o_w
    outs = []
    for g in range(KV_GROUPS):
        ocat = jnp.concatenate([o[(g * GROUP_SIZE + r) * t_new:(g * GROUP_SIZE + r + 1) * t_new]
                                for r in range(GROUP_SIZE)], axis=1).astype(BF16)
        outs.append(_bdot(ocat, move_ref[g]))
    o_ref[...] = jnp.concatenate(outs, axis=1)

    shifted = pltpu.roll(st_ref[...], wb - t_new, 1)
    new_t = jnp.transpose(jnp.concatenate([jnp.zeros((LANES - t_new, 2 * GD), F32), winn], axis=0))
    last = jnp.where(_lane_iota((2 * GD, LANES)) < LANES - t_new, shifted[:, wb - LANES:], new_t)
    nwin_ref[:, :wb - LANES] = shifted[:, :wb - LANES]
    nwin_ref[:, wb - LANES:] = last


def nsa_sample(cache_t, li, page_table, state_win_t, qx_s, kvn, winn, gates_s, cmp_w, past):
    pe_k, w1k, w2k, pe_v, w1v, w2v = cmp_w
    n_seq, n_pages = page_table.shape
    t_new = qx_s.shape[0] // n_seq
    wb = state_win_t.shape[3]
    assert wb >= LANES
    nch = past // CMP_STRIDE
    n_sel = -(-(past + t_new) // SEL_BLOCK)
    ov = _overlap_matrix(nch, n_sel)
    src = jnp.arange(GROUP_SIZE * GD)[None, :, None]
    dst = jnp.arange(GD)[None, None, :]
    grp = jnp.arange(KV_GROUPS)[:, None, None]
    move = (src == (dst // HEAD_DIM) * GD + grp * HEAD_DIM + dst % HEAD_DIM).astype(BF16)
    consts = (w1k, w1v, pe_k.reshape(1, -1), pe_v.reshape(1, -1), _w2_padded(w2k), _w2_padded(w2v), ov, move)
    full = lambda a: pl.BlockSpec(a.shape, lambda b, pt: (0,) * a.ndim)
    seq_rows = lambda w: pl.BlockSpec((t_new, w), lambda b, pt: (b, 0))
    page_spec = lambda p: pl.BlockSpec((None, None, D, PAGE), lambda b, pt: (li, pt[b, p], 0, 0))
    grid_spec = pltpu.PrefetchScalarGridSpec(
        num_scalar_prefetch=1, grid=(n_seq,),
        in_specs=[page_spec(p) for p in range(n_pages)]
        + [pl.BlockSpec((None, None, 2 * GD, wb), lambda b, pt: (li, b, 0, 0)),
           seq_rows(N_HEADS * GD), seq_rows(D), seq_rows(2 * GD), seq_rows(LANES)]
        + [full(a) for a in consts],
        out_specs=[seq_rows(D), pl.BlockSpec((None, 2 * GD, wb), lambda b, pt: (b, 0, 0))],
        scratch_shapes=[pltpu.VMEM((2, KV_GROUPS, nch, 2 * CMP_HIDDEN), F32),
                        pltpu.VMEM((2, GD // LANES, past, LANES), F32)],
    )
    return pl.pallas_call(
        functools.partial(_nsa_sample_kernel, n_pages, past), grid_spec=grid_spec,
        out_shape=[jax.ShapeDtypeStruct((n_seq * t_new, D), F32), jax.ShapeDtypeStruct((n_seq, 2 * GD, wb), F32)],
        compiler_params=_cparams(1), name="nsa_sample",
    )(page_table, *([cache_t] * n_pages), state_win_t, qx_s, kvn, winn, gates_s, *consts)


def _moe_chunks(n):
    chunks = 1
    while n // chunks > 2304 and (n // chunks) % 16 == 0:
        chunks *= 2
    return chunks


def kernel(x_prompt, x_sample, cache_kv, state_win, state_conv, state_h, page_table, p_prompt, p_sample, ln_g, ln_b, rg_w_in, rg_conv_w, rg_conv_b, rg_w_a, rg_b_a, rg_w_i, rg_b_i, rg_lam, rg_w_o, nsa_w_in, nsa_pe_k, nsa_ck_w1, nsa_ck_w2, nsa_pe_v, nsa_cv_w1, nsa_cv_w2, nsa_w_o, moe_w_r, moe_b_r, moe_w_gu, moe_b_gu, moe_w_dn, moe_b_dn, ple_w_proj, ple_w_gate):
    batch, seq = x_prompt.shape[:2]
    n_seq, t_new = x_sample.shape[:2]
    depth = ln_g.shape[0]
    past = page_table.shape[1] * PAGE
    wb = state_win.shape[2]
    assert seq >= wb and wb == min(WINDOW, past)
    alpha = (2 * depth) ** 0.25
    n_p, n_s = batch * seq, n_seq * t_new
    n = n_p + n_s
    x = jnp.concatenate([x_prompt.reshape(n_p, D), x_sample.reshape(n_s, D)], axis=0)
    p_all = jnp.concatenate([p_prompt.reshape(depth, n_p, PLE_DIM), p_sample.reshape(depth, n_s, PLE_DIM)], axis=1)
    pos = jnp.concatenate([jnp.tile(jnp.arange(seq), batch), past + jnp.tile(jnp.arange(t_new), n_seq)])
    cache_t = jnp.transpose(cache_kv, (0, 1, 3, 4, 5, 2)).reshape(cache_kv.shape[0], cache_kv.shape[1], D, PAGE)
    win_t = jnp.transpose(state_win, (0, 1, 3, 4, 5, 2)).reshape(state_win.shape[0], n_seq, 2 * GD, wb)
    w_gu16 = moe_w_gu.astype(BF16)
    w_dn16 = moe_w_dn.astype(BF16)
    n_chunks = _moe_chunks(n)
    kv_p, kv_s, win_p, win_s, conv_p, conv_s, h_p, h_s = [], [], [], [], [], [], [], []
    for i in range(depth):
        li = i // 2
        if i % 2 == 0:
            weights = (rg_w_in[li], rg_conv_w[li], rg_conv_b[li], rg_w_a[li], rg_b_a[li], rg_w_i[li], rg_b_i[li],
                       rg_lam[li])
            hy_p, cp, hp = rg_prompt(x[:n_p], batch, seq, weights)
            hy_s, cs, hs = rg_sample(x[n_p:], n_seq, t_new, state_conv[li], state_h[li], weights)
            conv_p.append(cp)
            conv_s.append(cs)
            h_p.append(hp)
            h_s.append(hs)
            mixed = jnp.concatenate([hy_p, hy_s], axis=0)
            w_o = rg_w_o[li]
        else:
            qx, kv32, win32, kv16, win16, vt_all, gates = nsa_proj(x, pos, nsa_w_in[li])
            cmp_w = (nsa_pe_k[li], nsa_ck_w1[li], nsa_ck_w2[li], nsa_pe_v[li], nsa_cv_w1[li], nsa_cv_w2[li])
            kc, vct = cmp_prompt(kv16[:n_p], batch, seq, cmp_w)
            o_p = attn_prompt_t(qx[:n_p], kc, vct, kv16[:n_p], win16[:n_p], vt_all, gates[:n_p], batch, seq)
            o_s, nwin = nsa_sample(cache_t, li, page_table, win_t, qx[n_p:].astype(F32), kv32[n_p:], win32[n_p:],
                                   gates[n_p:], cmp_w, past)
            kv_p.append(kv32[:n_p].reshape(batch, seq, 4, KV_GROUPS, HEAD_DIM))
            kv_s.append(kv32[n_p:].reshape(n_seq, t_new, 4, KV_GROUPS, HEAD_DIM))
            win_p.append(win32[:n_p].reshape(batch, seq, 2, KV_GROUPS, HEAD_DIM)[:, seq - wb:])
            win_s.append(jnp.transpose(nwin.reshape(n_seq, 2, KV_GROUPS, HEAD_DIM, wb), (0, 4, 1, 2, 3)))
            mixed = jnp.concatenate([o_p, o_s.astype(BF16)], axis=0)
            w_o = nsa_w_o[li]
        x1, route = mix_ln_router(x, mixed, w_o, ln_g[i, 0], ln_b[i, 0], moe_w_r[i], moe_b_r[i], alpha)
        moe = moe_ffn(x1, route, i, w_gu16, moe_b_gu, w_dn16, moe_b_dn, n_chunks)
        x = ln_ple(x1, moe, p_all[i], ln_g[i, 1], ln_b[i, 1], ple_w_gate[i], ple_w_proj[i], alpha)
    return (x[:n_p].reshape(batch, seq, D), x[n_p:].reshape(n_seq, t_new, D),
            jnp.stack(kv_p), jnp.stack(kv_s), jnp.stack(win_p), jnp.stack(win_s),
            jnp.stack(conv_p), jnp.stack(conv_s), jnp.stack(h_p), jnp.stack(h_s))
```
